```python
import math
import jax, jax.numpy as jnp
from jax import lax
import numpy as np

D_MODEL = 1024
BATCH = 32
SEQ = 2048
DEPTH = 4
DEC_BATCH = 16
DEC_SEQ = 16
PAST_LEN = 2048

CHUNK = 64
Q_BLOCK = 128
CONV_DIM = D_MODEL // 2
CONV_WIDTH = 31
ATT_DIM = D_MODEL - CONV_DIM
HEAD_DIM = 64
N_ATT_HEADS = ATT_DIM // (2 * HEAD_DIM)
IN_DIM = 2 * CONV_DIM + 3 * ATT_DIM
ROPE_THETA = 10000.0
D_FF = 2816
N_EXPERTS = 8
TOP_K = 2
D_FF_EXPERT = 3584
N_DENSE = (DEPTH + 1) // 2
N_MOE = DEPTH // 2
EPS = 1e-6
ATT_SCALE = HEAD_DIM ** -0.5
NEG_INF = -1e30

kernel_name = "hybrid_conv_diffattn_stream_step"


def rms_norm(x, g):
    x32 = x.astype(jnp.float32)
    y = x32 * lax.rsqrt(jnp.mean(x32 * x32, axis=-1, keepdims=True) + EPS)
    return (y * g.astype(jnp.float32)).astype(x.dtype)


def layer_norm(x, g, b):
    x32 = x.astype(jnp.float32)
    mu = jnp.mean(x32, axis=-1, keepdims=True)
    var = jnp.mean(jnp.square(x32 - mu), axis=-1, keepdims=True)
    y = (x32 - mu) * lax.rsqrt(var + EPS)
    return (y * g.astype(jnp.float32) + b.astype(jnp.float32)).astype(x.dtype)


def adaln(c, w, b):
    mod = jax.nn.silu(c) @ w + b
    return [m[:, None, :] for m in jnp.split(mod, 6, axis=-1)]


def modulate(h, shift, scale):
    return h * (1 + scale) + shift


def rotary(x, pos):
    half = x.shape[-1] // 2
    inv_freq = 1.0 / (ROPE_THETA ** (jnp.arange(half, dtype=jnp.float32) / half))
    ang = pos.astype(jnp.float32)[:, None] * inv_freq[None, :]
    cos = jnp.cos(ang)[None, :, None, :]
    sin = jnp.sin(ang)[None, :, None, :]
    x32 = x.astype(jnp.float32)
    x1, x2 = x32[..., :half], x32[..., half:]
    return jnp.concatenate([x1 * cos - x2 * sin, x2 * cos + x1 * sin], axis=-1).astype(x.dtype)


def causal_depthwise(upad, w, b):
    y = lax.conv_general_dilated(upad, w[:, None, :], window_strides=(1,), padding='VALID',
                                 dimension_numbers=('NWC', 'WIO', 'NWC'),
                                 feature_group_count=upad.shape[-1])
    return y + b


def diff_attn_core(q, k, v, mask, lam):
    B, Tq, N2, _ = q.shape
    Tk = k.shape[1]
    s = jnp.einsum('bqnd,bknd->bnqk', q, k, preferred_element_type=jnp.float32) * ATT_SCALE
    if mask is not None:
        s = jnp.where(mask[None, None], s, NEG_INF)
    p = jax.nn.softmax(s, axis=-1).reshape(B, N2 // 2, 2, Tq, Tk)
    a = p[:, :, 0] - lam * p[:, :, 1]
    return jnp.einsum('bhqk,bkhe->bqhe', a.astype(v.dtype), v)


def diff_attn_prompt(q, k, v, lam):
    T = q.shape[1]
    outs = []
    for i in range(T // Q_BLOCK):
        s0, e0 = i * Q_BLOCK, (i + 1) * Q_BLOCK
        qc = jnp.arange(s0, e0) // CHUNK
        kc = jnp.arange(e0) // CHUNK
        mask = qc[:, None] >= kc[None, :]
        outs.append(diff_attn_core(q[:, s0:e0], k[:, :e0], v[:, :e0], mask, lam))
    return jnp.concatenate(outs, axis=1)


def swiglu(h, wg, wu, wd):
    return (jax.nn.silu(h @ wg) * (h @ wu)) @ wd


def moe_ffn(h, w_router, wg, wu, wd):
    logits = jnp.einsum('btd,de->bte', h, w_router, preferred_element_type=jnp.float32)
    top_val, top_idx = lax.top_k(logits, TOP_K)
    top_w = jax.nn.softmax(top_val, axis=-1)
    gates = jnp.sum(jax.nn.one_hot(top_idx, N_EXPERTS, dtype=jnp.float32) * top_w[..., None], axis=-2)
    y = jnp.zeros_like(h)
    for e in range(N_EXPERTS):
        y = y + gates[..., e:e + 1].astype(h.dtype) * swiglu(h, wg[e], wu[e], wd[e])
    return y


def setup_inputs(seed: int = 0) -> dict:
    key = jax.random.key(seed)
    ks = iter(jax.random.split(key, 40))

    def nrm(shape, scale=1.0):
        return jax.random.normal(next(ks), shape, jnp.float32) * scale

    return {
        "x_prompt": nrm((BATCH, SEQ, D_MODEL)),
        "x_sample": nrm((DEC_BATCH, DEC_SEQ, D_MODEL)),
        "c_prompt": nrm((BATCH, D_MODEL)),
        "c_sample": nrm((DEC_BATCH, D_MODEL)),
        "cache_k": nrm((DEPTH, DEC_BATCH, PAST_LEN, 2 * N_ATT_HEADS, HEAD_DIM)),
        "cache_v": nrm((DEPTH, DEC_BATCH, PAST_LEN, N_ATT_HEADS, 2 * HEAD_DIM)),
        "state_conv": nrm((DEPTH, DEC_BATCH, CONV_WIDTH - 1, CONV_DIM), 0.5),
        "w_ada": nrm((DEPTH, D_MODEL, 6 * D_MODEL), 0.5 * D_MODEL ** -0.5),
        "b_ada": nrm((DEPTH, 6 * D_MODEL), 0.02),
        "g_mix": 1.0 + nrm((DEPTH, D_MODEL), 0.02),
        "g_ffn": 1.0 + nrm((DEPTH, D_MODEL), 0.02),
        "w_in": nrm((DEPTH, D_MODEL, IN_DIM), D_MODEL ** -0.5),
        "w_dw": nrm((DEPTH, CONV_WIDTH, CONV_DIM), CONV_WIDTH ** -0.5),
        "b_dw": nrm((DEPTH, CONV_DIM), 0.02),
        "g_conv_ln": 1.0 + nrm((DEPTH, CONV_DIM), 0.02),
        "b_conv_ln": nrm((DEPTH, CONV_DIM), 0.02),
        "lambda_q1": nrm((DEPTH, HEAD_DIM), 0.1),
        "lambda_k1": nrm((DEPTH, HEAD_DIM), 0.1),
        "lambda_q2": nrm((DEPTH, HEAD_DIM), 0.1),
        "lambda_k2": nrm((DEPTH, HEAD_DIM), 0.1),
        "g_subln": 1.0 + nrm((DEPTH, 2 * HEAD_DIM), 0.02),
        "w_out": nrm((DEPTH, D_MODEL, D_MODEL), D_MODEL ** -0.5),
        "w_ffn_gate": nrm((N_DENSE, D_MODEL, D_FF), D_MODEL ** -0.5),
        "w_ffn_up": nrm((N_DENSE, D_MODEL, D_FF), D_MODEL ** -0.5),
        "w_ffn_down": nrm((N_DENSE, D_FF, D_MODEL), D_FF ** -0.5),
        "w_router": nrm((N_MOE, D_MODEL, N_EXPERTS), D_MODEL ** -0.5),
        "w_moe_gate": nrm((N_MOE, N_EXPERTS, D_MODEL, D_FF_EXPERT), D_MODEL ** -0.5),
        "w_moe_up": nrm((N_MOE, N_EXPERTS, D_MODEL, D_FF_EXPERT), D_MODEL ** -0.5),
        "w_moe_down": nrm((N_MOE, N_EXPERTS, D_FF_EXPERT, D_MODEL), D_FF_EXPERT ** -0.5),
        "g_final": 1.0 + nrm((D_MODEL,), 0.02),
    }


def reference(x_prompt, x_sample, c_prompt, c_sample, cache_k, cache_v, state_conv,
              w_ada, b_ada, g_mix, g_ffn, w_in, w_dw, b_dw, g_conv_ln, b_conv_ln,
              lambda_q1, lambda_k1, lambda_q2, lambda_k2, g_subln, w_out,
              w_ffn_gate, w_ffn_up, w_ffn_down, w_router, w_moe_gate, w_moe_up, w_moe_down,
              g_final):

    def layer(l, x, c, pos, conv_past, k_past, v_past):
        shift1, scale1, gate1, shift2, scale2, gate2 = adaln(c, w_ada[l], b_ada[l])
        B, T, _ = x.shape
        h = modulate(rms_norm(x, g_mix[l]), shift1, scale1)
        z = h @ w_in[l]
        u = z[..., :CONV_DIM] * jax.nn.sigmoid(z[..., CONV_DIM:2 * CONV_DIM])
        upad = jnp.concatenate([conv_past, u], axis=1)
        new_conv = upad[:, -(CONV_WIDTH - 1):]
        cv = causal_depthwise(upad, w_dw[l], b_dw[l])
        cv = jax.nn.silu(layer_norm(cv, g_conv_ln[l], b_conv_ln[l]))
        o = 2 * CONV_DIM
        q = z[..., o:o + ATT_DIM].reshape(B, T, 2 * N_ATT_HEADS, HEAD_DIM)
        k = z[..., o + ATT_DIM:o + 2 * ATT_DIM].reshape(B, T, 2 * N_ATT_HEADS, HEAD_DIM)
        v = z[..., o + 2 * ATT_DIM:].reshape(B, T, N_ATT_HEADS, 2 * HEAD_DIM)
        q = rotary(q, pos)
        k = rotary(k, pos)
        lam_init = 0.8 - 0.6 * math.exp(-0.3 * l)
        f32 = jnp.float32
        lam = (jnp.exp(jnp.sum(lambda_q1[l].astype(f32) * lambda_k1[l].astype(f32)))
               - jnp.exp(jnp.sum(lambda_q2[l].astype(f32) * lambda_k2[l].astype(f32))) + lam_init)
        if k_past is None:
            att = diff_attn_prompt(q, k, v, lam)
        else:
            att = diff_attn_core(q, jnp.concatenate([k_past, k], axis=1),
                                 jnp.concatenate([v_past, v], axis=1), None, lam)
        att = rms_norm(att, g_subln[l]) * (1.0 - lam_init)
        mix = jnp.concatenate([cv, att.reshape(B, T, ATT_DIM)], axis=-1) @ w_out[l]
        x = x + gate1 * mix
        h = modulate(rms_norm(x, g_ffn[l]), shift2, scale2)
        if l % 2 == 0:
            i = l // 2
            f = swiglu(h, w_ffn_gate[i], w_ffn_up[i], w_ffn_down[i])
        else:
            i = l // 2
            f = moe_ffn(h, w_router[i], w_moe_gate[i], w_moe_up[i], w_moe_down[i])
        x = x + gate2 * f
        return x, new_conv, k, v

    past_len = cache_k.shape[2]
    pos_p = jnp.arange(x_prompt.shape[1])
    pos_s = past_len + jnp.arange(x_sample.shape[1])
    xp, xs = x_prompt, x_sample
    kp_l, vp_l, cp_l, ks_l, vs_l, cs_l = [], [], [], [], [], []
    for l in range(DEPTH):
        zero_conv = jnp.zeros((xp.shape[0], CONV_WIDTH - 1, CONV_DIM), xp.dtype)
        xp, cp, kp, vp = layer(l, xp, c_prompt, pos_p, zero_conv, None, None)
        xs, cs, ks, vs = layer(l, xs, c_sample, pos_s, state_conv[l], cache_k[l], cache_v[l])
        kp_l.append(kp); vp_l.append(vp); cp_l.append(cp)
        ks_l.append(ks); vs_l.append(vs); cs_l.append(cs)

    y_prompt = rms_norm(xp, g_final)
    y_sample = rms_norm(xs, g_final)
    new_k_prompt = jnp.stack(kp_l)
    new_v_prompt = jnp.stack(vp_l)
    new_conv_prompt = jnp.stack(cp_l)
    new_k_sample = jnp.stack(ks_l)
    new_v_sample = jnp.stack(vs_l)
    new_conv_sample = jnp.stack(cs_l)
    return (y_prompt, y_sample, new_k_prompt, new_v_prompt, new_conv_prompt,
            new_k_sample, new_v_sample, new_conv_sample)
```

```python
import functools
import math

import jax
import jax.numpy as jnp
from jax import lax
from jax.experimental import pallas as pl
from jax.experimental.pallas import tpu as pltpu

F32 = jnp.float32
BF16 = jnp.bfloat16

EPS = 1e-6
CHUNK = 64
ROPE_THETA = 10000.0
NEG_INF = -1e30

LANES = 128
SUBLANES = 8
VMEM_LIMIT_BYTES = 56 * 1024 * 1024

CONV_HALO = 32
Q_BLOCK = 256
ADA_COLS = 1536


def _cparams(sem):
    return pltpu.CompilerParams(dimension_semantics=sem, vmem_limit_bytes=VMEM_LIMIT_BYTES)


def _sigmoid(x):
    return 1.0 / (1.0 + jnp.exp(-x))


def _silu(x):
    return x * _sigmoid(x)


def _rms(x, g):
    return x * lax.rsqrt(jnp.mean(x * x, axis=-1, keepdims=True) + EPS) * g


def _adaln_kernel(c_ref, w_ref, b_ref, o_ref):
    s = _silu(c_ref[...]).astype(BF16)
    o_ref[...] = jnp.dot(s, w_ref[...].astype(BF16), preferred_element_type=F32) + b_ref[...]


def _adaln(c_all, w_ada, b_ada):
    depth, d, n = w_ada.shape
    rows = c_all.shape[0]
    nb = ADA_COLS if n % ADA_COLS == 0 else n
    return pl.pallas_call(
        _adaln_kernel,
        grid=(depth, n // nb),
        in_specs=[
            pl.BlockSpec((rows, d), lambda l, j: (0, 0)),
            pl.BlockSpec((None, d, nb), lambda l, j: (l, 0, j)),
            pl.BlockSpec((None, 1, nb), lambda l, j: (l, 0, j)),
        ],
        out_specs=pl.BlockSpec((None, rows, nb), lambda l, j: (l, 0, j)),
        out_shape=jax.ShapeDtypeStruct((depth, rows, n), F32),
        compiler_params=_cparams(("arbitrary", "arbitrary")),
        name="adaln",
    )(c_all, w_ada, b_ada.reshape(depth, 1, n))


def _mixer_in_kernel(x_ref, shift_ref, scale_ref, g_ref, w_ref, cos_ref, sin_ref, hist_ref,
                     wdw_ref, bdw_ref, gln_ref, bln_ref,
                     k_ref, v_ref, nc_ref, qb_ref, kb_ref, vx_ref, cv_ref,
                     ext_ref, craw_ref, *, tm, conv_dim, att_dim, head_dim, width, att_scale,
                     transpose_v, row_chunk):
    t = pl.program_id(1)
    nt = pl.num_programs(1)

    @pl.when(t == 0)
    def _():
        ext_ref[0:CONV_HALO, :] = hist_ref[...]

    x = x_ref[...]
    h = _rms(x, g_ref[...]) * (1.0 + scale_ref[...]) + shift_ref[...]
    z = jnp.dot(h.astype(BF16), w_ref[...], preferred_element_type=F32)

    u = z[:, :conv_dim] * _sigmoid(z[:, conv_dim:2 * conv_dim])
    ext_ref[CONV_HALO:CONV_HALO + tm, :] = u
    first_tap = CONV_HALO - (width - 1)

    for r0 in range(0, tm, row_chunk):
        for lb in range(conv_dim // LANES):
            sl = slice(lb * LANES, (lb + 1) * LANES)
            acc = jnp.broadcast_to(bdw_ref[:, sl], (row_chunk, LANES))
            for w in range(width):
                a0 = r0 + first_tap + w
                acc = acc + ext_ref[a0:a0 + row_chunk, sl] * wdw_ref[w:w + 1, sl]
            craw_ref[r0:r0 + row_chunk, sl] = acc

    cr = craw_ref[...]
    mu = jnp.mean(cr, axis=-1, keepdims=True)
    dlt = cr - mu
    var = jnp.mean(dlt * dlt, axis=-1, keepdims=True)
    y = dlt * lax.rsqrt(var + EPS) * gln_ref[...] + bln_ref[...]
    cv_ref[...] = _silu(y).astype(BF16)

    @pl.when(t == nt - 1)
    def _():
        nc_ref[...] = ext_ref[tm + first_tap:tm + CONV_HALO, :]

    tail = ext_ref[tm:tm + CONV_HALO, :]
    ext_ref[0:CONV_HALO, :] = tail

    lane = lax.broadcasted_iota(jnp.int32, (1, LANES), 1)
    first_half = (lane & (head_dim - 1)) < (head_dim // 2)
    cos = cos_ref[...]
    sin = sin_ref[...]

    def rope(xb):
        partner = jnp.where(first_half,
                            pltpu.roll(xb, LANES - head_dim // 2, 1),
                            pltpu.roll(xb, head_dim // 2, 1))
        return xb * cos + partner * sin

    o = 2 * conv_dim
    for lb in range(att_dim // LANES):
        sl = slice(lb * LANES, (lb + 1) * LANES)
        qblk = rope(z[:, o + lb * LANES:o + (lb + 1) * LANES])
        qb_ref[:, sl] = (qblk * att_scale).astype(BF16)
        kblk = rope(z[:, o + att_dim + lb * LANES:o + att_dim + (lb + 1) * LANES])
        k_ref[:, sl] = kblk
        kb_ref[:, sl] = kblk.astype(BF16)
    v = z[:, o + 2 * att_dim:]
    v_ref[...] = v
    if transpose_v:
        vx_ref[...] = v.T.astype(BF16)
    else:
        vx_ref[...] = v.astype(BF16)


def _mixer_in(x, mod, g, w_in_b, cos_t, sin_t, hist, w_dw, b_dw, g_ln, b_ln, *, head_dim, transpose_v):
    b, t_len, d = x.shape
    conv_dim = w_dw.shape[1]
    width = w_dw.shape[0]
    att_dim = (w_in_b.shape[1] - 2 * conv_dim) // 3
    tm = min(512, t_len)
    row_chunk = min(32, tm)
    nt = t_len // tm
    kern = functools.partial(
        _mixer_in_kernel, tm=tm, conv_dim=conv_dim, att_dim=att_dim, head_dim=head_dim, width=width,
        att_scale=head_dim ** -0.5, transpose_v=transpose_v, row_chunk=row_chunk)
    row = lambda bi, ti: (bi, ti, 0)
    vec = lambda bi, ti: (0, 0)
    if transpose_v:
        vx_spec = pl.BlockSpec((None, att_dim, tm), lambda bi, ti: (bi, 0, ti))
        vx_shape = jax.ShapeDtypeStruct((b, att_dim, t_len), BF16)
    else:
        vx_spec = pl.BlockSpec((None, tm, att_dim), row)
        vx_shape = jax.ShapeDtypeStruct((b, t_len, att_dim), BF16)
    return pl.pallas_call(
        kern,
        grid=(b, nt),
        in_specs=[
            pl.BlockSpec((None, tm, d), row),
            pl.BlockSpec((None, None, 1, d), lambda bi, ti: (bi, 0, 0, 0)),
            pl.BlockSpec((None, None, 1, d), lambda bi, ti: (bi, 1, 0, 0)),
            pl.BlockSpec((1, d), vec),
            pl.BlockSpec(w_in_b.shape, vec),
            pl.BlockSpec((tm, LANES), lambda bi, ti: (ti, 0)),
            pl.BlockSpec((tm, LANES), lambda bi, ti: (ti, 0)),
            pl.BlockSpec((None, CONV_HALO, conv_dim), lambda bi, ti: (bi, 0, 0)),
            pl.BlockSpec((width, conv_dim), vec),
            pl.BlockSpec((1, conv_dim), vec),
            pl.BlockSpec((1, conv_dim), vec),
            pl.BlockSpec((1, conv_dim), vec),
        ],
        out_specs=[
            pl.BlockSpec((None, tm, att_dim), row),
            pl.BlockSpec((None, tm, att_dim), row),
            pl.BlockSpec((None, width - 1, conv_dim), lambda bi, ti: (bi, 0, 0)),
            pl.BlockSpec((None, tm, att_dim), row),
            pl.BlockSpec((None, tm, att_dim), row),
            vx_spec,
            pl.BlockSpec((None, tm, conv_dim), row),
        ],
        out_shape=[
            jax.ShapeDtypeStruct((b, t_len, att_dim), F32),
            jax.ShapeDtypeStruct((b, t_len, att_dim), F32),
            jax.ShapeDtypeStruct((b, width - 1, conv_dim), F32),
            jax.ShapeDtypeStruct((b, t_len, att_dim), BF16),
            jax.ShapeDtypeStruct((b, t_len, att_dim), BF16),
            vx_shape,
            jax.ShapeDtypeStruct((b, t_len, conv_dim), BF16),
        ],
        scratch_shapes=[
            pltpu.VMEM((CONV_HALO + tm, conv_dim), F32),
            pltpu.VMEM((tm, conv_dim), F32),
        ],
        compiler_params=_cparams(("arbitrary", "arbitrary")),
        name="mixer_in",
    )(x, mod, mod, g, w_in_b, cos_t, sin_t, hist, w_dw, b_dw, g_ln, b_ln)


def _lambda(lq1_ref, lk1_ref, lq2_ref, lk2_ref, lam_init):
    a = jnp.sum(lq1_ref[...] * lk1_ref[...], axis=-1, keepdims=True)
    b = jnp.sum(lq2_ref[...] * lk2_ref[...], axis=-1, keepdims=True)
    return jnp.exp(a) - jnp.exp(b) + lam_init


def _attn_prompt_kernel(q_ref, k_ref, vt_ref, lq1_ref, lk1_ref, lq2_ref, lk2_ref, gsub_ref, o_ref,
                        *, t_len, head_dim, lam_init):
    lam = _lambda(lq1_ref, lk1_ref, lq2_ref, lk2_ref, lam_init)
    n_pairs = q_ref.shape[-1] // LANES
    lane = lax.broadcasted_iota(jnp.int32, (1, LANES), 1)
    low = lane < head_dim
    kc = lax.broadcasted_iota(jnp.int32, (Q_BLOCK, Q_BLOCK), 0) // CHUNK
    qc = lax.broadcasted_iota(jnp.int32, (Q_BLOCK, Q_BLOCK), 1) // CHUNK
    visible = kc <= qc
    nt_dims = (((1,), (1,)), ((), ()))
    for j in range(t_len // Q_BLOCK):
        tk = (j + 1) * Q_BLOCK
        qs = slice(j * Q_BLOCK, tk)
        for hp in range(n_pairs):
            sl = slice(hp * LANES, (hp + 1) * LANES)
            kp = k_ref[0:tk, sl]
            qp = q_ref[qs, sl]
            vt = vt_ref[sl, 0:tk]
            parts = []
            for sub in range(2):
                qm = jnp.where(low if sub == 0 else jnp.logical_not(low), qp, jnp.zeros_like(qp))
                s = lax.dot_general(kp, qm, nt_dims, preferred_element_type=F32)
                diag = jnp.where(visible, s[j * Q_BLOCK:, :], NEG_INF)
                s = diag if j == 0 else jnp.concatenate([s[:j * Q_BLOCK, :], diag], axis=0)
                m = jnp.max(s, axis=0, keepdims=True)
                e = jnp.exp(s - m)
                l = jnp.sum(e, axis=0, keepdims=True)
                o = jnp.dot(vt, e.astype(BF16), preferred_element_type=F32)
                parts.append((o, l))
            ot = parts[0][0] * (1.0 / parts[0][1]) - parts[1][0] * (lam / parts[1][1])
            ms = jnp.mean(ot * ot, axis=0, keepdims=True)
            on = (ot * lax.rsqrt(ms + EPS)).T * (gsub_ref[...] * (1.0 - lam_init))
            o_ref[qs, sl] = on.astype(BF16)


def _attn_prompt(qb, kb, vt, lq1, lk1, lq2, lk2, gsub, *, head_dim, lam_init):
    b, t_len, a = qb.shape
    kern = functools.partial(_attn_prompt_kernel, t_len=t_len, head_dim=head_dim, lam_init=lam_init)
    vec = lambda bi: (0, 0)
    return pl.pallas_call(
        kern,
        grid=(b,),
        in_specs=[
            pl.BlockSpec((None, t_len, a), lambda bi: (bi, 0, 0)),
            pl.BlockSpec((None, t_len, a), lambda bi: (bi, 0, 0)),
            pl.BlockSpec((None, a, t_len), lambda bi: (bi, 0, 0)),
            pl.BlockSpec((1, head_dim), vec), pl.BlockSpec((1, head_dim), vec),
            pl.BlockSpec((1, head_dim), vec), pl.BlockSpec((1, head_dim), vec),
            pl.BlockSpec((1, 2 * head_dim), vec),
        ],
        out_specs=pl.BlockSpec((None, t_len, a), lambda bi: (bi, 0, 0)),
        out_shape=jax.ShapeDtypeStruct((b, t_len, a), BF16),
        compiler_params=_cparams(("arbitrary",)),
        name="attn_prompt",
    )(qb, kb, vt, lq1, lk1, lq2, lk2, gsub)


def _attn_sample_kernel(q_ref, kn_ref, vn_ref, kc_ref, vc_ref, lq1_ref, lk1_ref, lq2_ref, lk2_ref, gsub_ref,
                        o_ref, *, head_dim, lam_init):
    lam = _lambda(lq1_ref, lk1_ref, lq2_ref, lk2_ref, lam_init)
    n_pairs = q_ref.shape[-1] // LANES
    lane = lax.broadcasted_iota(jnp.int32, (1, LANES), 1)
    low = lane < head_dim
    nt_dims = (((1,), (1,)), ((), ()))
    for hp in range(n_pairs):
        sl = slice(hp * LANES, (hp + 1) * LANES)
        qp = q_ref[:, sl]
        kc = kc_ref[:, sl].astype(BF16)
        vc = vc_ref[:, sl].astype(BF16)
        kn = kn_ref[:, sl]
        vn = vn_ref[:, sl]
        parts = []
        for sub in range(2):
            qm = jnp.where(low if sub == 0 else jnp.logical_not(low), qp, jnp.zeros_like(qp))
            sc = lax.dot_general(qm, kc, nt_dims, preferred_element_type=F32)
            sn = lax.dot_general(qm, kn, nt_dims, preferred_element_type=F32)
            m = jnp.maximum(jnp.max(sc, axis=-1, keepdims=True), jnp.max(sn, axis=-1, keepdims=True))
            ec = jnp.exp(sc - m)
            en = jnp.exp(sn - m)
            l = jnp.sum(ec, axis=-1, keepdims=True) + jnp.sum(en, axis=-1, keepdims=True)
            o = (jnp.dot(ec.astype(BF16), vc, preferred_element_type=F32)
                 + jnp.dot(en.astype(BF16), vn, preferred_element_type=F32))
            parts.append((o, l))
        oh = parts[0][0] * (1.0 / parts[0][1]) - parts[1][0] * (lam / parts[1][1])
        on = _rms(oh, gsub_ref[...]) * (1.0 - lam_init)
        o_ref[:, sl] = on.astype(BF16)


def _attn_sample(qb, kb, vb, cache_k, cache_v, lq1, lk1, lq2, lk2, gsub, *, head_dim, lam_init):
    b, tq, a = qb.shape
    past = cache_k.shape[1]
    kern = functools.partial(_attn_sample_kernel, head_dim=head_dim, lam_init=lam_init)
    vec = lambda bi: (0, 0)
    row = lambda bi: (bi, 0, 0)
    return pl.pallas_call(
        kern,
        grid=(b,),
        in_specs=[
            pl.BlockSpec((None, tq, a), row), pl.BlockSpec((None, tq, a), row), pl.BlockSpec((None, tq, a), row),
            pl.BlockSpec((None, past, a), row), pl.BlockSpec((None, past, a), row),
            pl.BlockSpec((1, head_dim), vec), pl.BlockSpec((1, head_dim), vec),
            pl.BlockSpec((1, head_dim), vec), pl.BlockSpec((1, head_dim), vec),
            pl.BlockSpec((1, 2 * head_dim), vec),
        ],
        out_specs=pl.BlockSpec((None, tq, a), row),
        out_shape=jax.ShapeDtypeStruct((b, tq, a), BF16),
        compiler_params=_cparams(("arbitrary",)),
        name="attn_sample",
    )(qb, kb, vb, cache_k, cache_v, lq1, lk1, lq2, lk2, gsub)


def _out_proj_kernel(*refs, moe, n_experts):
    if moe:
        (cv_ref, att_ref, x_ref, gate_ref, shift_ref, scale_ref, g_ref, wo_ref, wr_ref,
         x1_ref, h2_ref, idx_ref, wts_ref) = refs
    else:
        cv_ref, att_ref, x_ref, gate_ref, shift_ref, scale_ref, g_ref, wo_ref, x1_ref, h2_ref = refs
    c = cv_ref.shape[-1]
    mix = (jnp.dot(cv_ref[...], wo_ref[0:c, :], preferred_element_type=F32)
           + jnp.dot(att_ref[...], wo_ref[c:, :], preferred_element_type=F32))
    x1 = x_ref[...] + gate_ref[...] * mix
    x1_ref[...] = x1
    h2 = _rms(x1, g_ref[...]) * (1.0 + scale_ref[...]) + shift_ref[...]
    h2_ref[...] = h2.astype(h2_ref.dtype)
    if moe:
        logits = jnp.dot(h2.astype(BF16), wr_ref[...], preferred_element_type=F32)
        lane = lax.broadcasted_iota(jnp.int32, logits.shape, 1)
        logits = jnp.where(lane < n_experts, logits, -jnp.inf)
        m1 = jnp.max(logits, axis=-1, keepdims=True)
        i1 = jnp.min(jnp.where(logits == m1, lane, LANES), axis=-1, keepdims=True)
        rest = jnp.where(lane == i1, -jnp.inf, logits)
        m2 = jnp.max(rest, axis=-1, keepdims=True)
        i2 = jnp.min(jnp.where(rest == m2, lane, LANES), axis=-1, keepdims=True)
        e2 = jnp.exp(m2 - m1)
        w1 = 1.0 / (1.0 + e2)
        w2 = e2 / (1.0 + e2)
        col = lax.broadcasted_iota(jnp.int32, idx_ref.shape, 1)
        idx_ref[...] = jnp.where(col == 0, i1, i2)
        wts_ref[...] = jnp.where(col == 0, w1, w2)


def _out_proj(cv, att, x, gate, shift, scale, g, wo_b, wr_b, *, per_row_mod, n_experts):
    moe = wr_b is not None
    b, t_len, d = x.shape
    c = cv.shape[-1]
    tm = min(512, t_len)
    nt = t_len // tm
    row = lambda bi, ti: (bi, ti, 0)
    vec = lambda bi, ti: (0, 0)

    def mod_spec(i):
        if per_row_mod:
            return pl.BlockSpec((None, None, tm, d), lambda bi, ti: (i, bi, ti, 0))
        return pl.BlockSpec((None, None, 1, d), lambda bi, ti: (bi, i, 0, 0))

    in_specs = [
        pl.BlockSpec((None, tm, c), row), pl.BlockSpec((None, tm, c), row), pl.BlockSpec((None, tm, d), row),
        mod_spec(2), mod_spec(3), mod_spec(4),
        pl.BlockSpec((1, d), vec), pl.BlockSpec(wo_b.shape, vec),
    ]
    args = [cv, att, x, gate, shift, scale, g, wo_b]
    out_specs = [pl.BlockSpec((None, tm, d), row), pl.BlockSpec((None, tm, d), row)]
    out_shape = [jax.ShapeDtypeStruct((b, t_len, d), F32),
                 jax.ShapeDtypeStruct((b, t_len, d), F32 if moe else BF16)]
    if moe:
        in_specs.append(pl.BlockSpec(wr_b.shape, vec))
        args.append(wr_b)
        out_specs += [pl.BlockSpec((None, tm, 2), row), pl.BlockSpec((None, tm, 2), row)]
        out_shape += [jax.ShapeDtypeStruct((b, t_len, 2), jnp.int32), jax.ShapeDtypeStruct((b, t_len, 2), F32)]
    return pl.pallas_call(
        functools.partial(_out_proj_kernel, moe=moe, n_experts=n_experts),
        grid=(b, nt),
        in_specs=in_specs,
        out_specs=out_specs,
        out_shape=out_shape,
        compiler_params=_cparams(("arbitrary", "arbitrary")),
        name="out_proj_router" if moe else "out_proj",
    )(*args)


def _dense_ffn_kernel(*refs, final):
    if final:
        h_ref, x1_ref, gate_ref, wg_ref, wu_ref, wd_ref, gf_ref, o_ref = refs
    else:
        h_ref, x1_ref, gate_ref, wg_ref, wu_ref, wd_ref, o_ref = refs
    hb = h_ref[...]
    gp = jnp.dot(hb, wg_ref[...], preferred_element_type=F32)
    up = jnp.dot(hb, wu_ref[...], preferred_element_type=F32)
    act = (_silu(gp) * up).astype(BF16)
    f = jnp.dot(act, wd_ref[...], preferred_element_type=F32)
    x = x1_ref[...] + gate_ref[...] * f
    o_ref[...] = _rms(x, gf_ref[...]) if final else x


def _dense_ffn(h2, x1, gate, wg_b, wu_b, wd_b, g_final, *, per_row_mod):
    b, t_len, d = x1.shape
    tm = min(256, t_len)
    nt = t_len // tm
    final = g_final is not None
    row = lambda bi, ti: (bi, ti, 0)
    vec = lambda bi, ti: (0, 0)
    if per_row_mod:
        gate_spec = pl.BlockSpec((None, None, tm, d), lambda bi, ti: (5, bi, ti, 0))
    else:
        gate_spec = pl.BlockSpec((None, None, 1, d), lambda bi, ti: (bi, 5, 0, 0))
    in_specs = [pl.BlockSpec((None, tm, d), row), pl.BlockSpec((None, tm, d), row), gate_spec,
                pl.BlockSpec(wg_b.shape, vec), pl.BlockSpec(wu_b.shape, vec), pl.BlockSpec(wd_b.shape, vec)]
    args = [h2, x1, gate, wg_b, wu_b, wd_b]
    if final:
        in_specs.append(pl.BlockSpec((1, d), vec))
        args.append(g_final)
    return pl.pallas_call(
        functools.partial(_dense_ffn_kernel, final=final),
        grid=(b, nt),
        in_specs=in_specs,
        out_specs=pl.BlockSpec((None, tm, d), row),
        out_shape=jax.ShapeDtypeStruct((b, t_len, d), F32),
        compiler_params=_cparams(("arbitrary", "arbitrary")),
        name="dense_ffn",
    )(*args)


def _moe_expert_kernel(te_ref, tv_ref, src_hbm, dst_hbm, h_hbm, wg_ref, wu_ref, wd_ref, ys_hbm,
                       src_smem, dst_smem, xs_ref, xb_ref, acc_ref, sems, *, tm):
    i = pl.program_id(0)
    j = pl.program_id(1)
    nj = pl.num_programs(1)

    def row_copy_in(r):
        return pltpu.make_async_copy(h_hbm.at[pl.ds(src_smem[0, r], 1)], xs_ref.at[pl.ds(r, 1)], sems.at[0])

    def row_copy_out(r):
        return pltpu.make_async_copy(acc_ref.at[pl.ds(r, 1)], ys_hbm.at[pl.ds(dst_smem[0, r], 1)], sems.at[1])

    @pl.when(tv_ref[i] > 0)
    def _():
        @pl.when(j == 0)
        def _():
            cs = pltpu.make_async_copy(src_hbm.at[pl.ds(i, 1)], src_smem, sems.at[2])
            cd = pltpu.make_async_copy(dst_hbm.at[pl.ds(i, 1)], dst_smem, sems.at[3])
            cs.start()
            cd.start()
            cs.wait()
            cd.wait()

            def issue(r, carry):
                row_copy_in(r).start()
                return carry

            lax.fori_loop(0, tm, issue, 0)
            pltpu.make_async_copy(h_hbm.at[pl.ds(0, tm)], xs_ref, sems.at[0]).wait()
            xb_ref[...] = xs_ref[...].astype(BF16)
            acc_ref[...] = jnp.zeros_like(acc_ref)

            @pl.when(i == 0)
            def _():
                n_dump = ys_hbm.shape[0] - tm
                fill = pltpu.make_async_copy(acc_ref, ys_hbm.at[pl.ds(n_dump, tm)], sems.at[1])
                fill.start()
                fill.wait()

        xb = xb_ref[...]
        gp = jnp.dot(xb, wg_ref[...].astype(BF16), preferred_element_type=F32)
        up = jnp.dot(xb, wu_ref[...].astype(BF16), preferred_element_type=F32)
        act = (_silu(gp) * up).astype(BF16)
        acc_ref[...] += jnp.dot(act, wd_ref[...].astype(BF16), preferred_element_type=F32)

        @pl.when(j == nj - 1)
        def _():
            def issue(r, carry):
                row_copy_out(r).start()
                return carry

            lax.fori_loop(0, tm, issue, 0)
            pltpu.make_async_copy(acc_ref, ys_hbm.at[pl.ds(0, tm)], sems.at[1]).wait()


def _moe_tile(n_pairs, n_experts):
    tile = 1024
    while tile > 128 and tile * n_experts > n_pairs:
        tile //= 2
    return tile


def _ff_chunk(dff):
    for c in (512, 256, 128):
        if dff % c == 0:
            return c
    return dff


def _moe_experts(h2, idx, wg, wu, wd):
    n, d = h2.shape
    n_experts, _, dff = wg.shape
    n_pairs = 2 * n
    tm = _moe_tile(n_pairs, n_experts)
    fc = _ff_chunk(dff)
    n_tiles = n_pairs // tm + n_experts
    rows = n_tiles * tm

    e_flat = idx.T.reshape(-1)
    order = jnp.argsort(e_flat, stable=True).astype(jnp.int32)
    counts = jnp.sum((e_flat[:, None] == jnp.arange(n_experts, dtype=jnp.int32)[None, :]).astype(jnp.int32), axis=0)
    padded = ((counts + tm - 1) // tm) * tm
    pad_end = jnp.cumsum(padded)
    pad_start = pad_end - padded
    cnt_start = jnp.cumsum(counts) - counts
    tile_first = jnp.arange(n_tiles, dtype=jnp.int32) * tm
    tile_valid = (tile_first < pad_end[-1]).astype(jnp.int32)
    tile_expert = jnp.minimum(jnp.searchsorted(pad_end, tile_first, side="right"), n_experts - 1).astype(jnp.int32)
    last_valid_expert = tile_expert[jnp.maximum(jnp.sum(tile_valid) - 1, 0)]
    tile_expert = jnp.where(tile_valid > 0, tile_expert, last_valid_expert)
    r = jnp.arange(rows, dtype=jnp.int32)
    e_r = jnp.repeat(tile_expert, tm)
    s = r - pad_start[e_r]
    row_valid = (s < counts[e_r]) & (jnp.repeat(tile_valid, tm) > 0)
    pair = order[jnp.clip(cnt_start[e_r] + s, 0, n_pairs - 1)]
    src = jnp.where(row_valid, pair % n, 0).astype(jnp.int32).reshape(n_tiles, tm)
    dst = jnp.where(row_valid, pair, n_pairs + r % tm).astype(jnp.int32).reshape(n_tiles, tm)

    grid_spec = pltpu.PrefetchScalarGridSpec(
        num_scalar_prefetch=2,
        grid=(n_tiles, dff // fc),
        in_specs=[
            pl.BlockSpec(memory_space=pl.ANY),
            pl.BlockSpec(memory_space=pl.ANY),
            pl.BlockSpec(memory_space=pl.ANY),
            pl.BlockSpec((None, d, fc), lambda i, j, te, tv: (te[i], 0, j)),
            pl.BlockSpec((None, d, fc), lambda i, j, te, tv: (te[i], 0, j)),
            pl.BlockSpec((None, fc, d), lambda i, j, te, tv: (te[i], j, 0)),
        ],
        out_specs=pl.BlockSpec(memory_space=pl.ANY),
        scratch_shapes=[
            pltpu.SMEM((1, tm), jnp.int32),
            pltpu.SMEM((1, tm), jnp.int32),
            pltpu.VMEM((tm, d), F32),
            pltpu.VMEM((tm, d), BF16),
            pltpu.VMEM((tm, d), F32),
            pltpu.SemaphoreType.DMA((4,)),
        ],
    )
    ys = pl.pallas_call(
        functools.partial(_moe_expert_kernel, tm=tm),
        grid_spec=grid_spec,
        out_shape=jax.ShapeDtypeStruct((n_pairs + tm, d), F32),
        compiler_params=_cparams(("arbitrary", "arbitrary")),
        name="moe_experts",
    )(tile_expert, tile_valid, src, dst, h2, wg, wu, wd)
    return ys


def _moe_combine_kernel(*refs, final):
    if final:
        y0_ref, y1_ref, wts_ref, x1_ref, gate_ref, gf_ref, o_ref = refs
    else:
        y0_ref, y1_ref, wts_ref, x1_ref, gate_ref, o_ref = refs
    w = wts_ref[...]
    y = w[:, 0:1] * y0_ref[...] + w[:, 1:2] * y1_ref[...]
    x = x1_ref[...] + gate_ref[...] * y
    o_ref[...] = _rms(x, gf_ref[...]) if final else x


def _moe_combine(ys, wts, x1, gate, g_final):
    b, t_len, d = x1.shape
    n = b * t_len
    tm = min(512, t_len)
    nt = t_len // tm
    final = g_final is not None
    row = lambda bi, ti: (bi, ti, 0)
    in_specs = [
        pl.BlockSpec((tm, d), lambda bi, ti: (bi * nt + ti, 0)),
        pl.BlockSpec((tm, d), lambda bi, ti: (n // tm + bi * nt + ti, 0)),
        pl.BlockSpec((None, tm, 2), row),
        pl.BlockSpec((None, tm, d), row),
        pl.BlockSpec((None, None, 1, d), lambda bi, ti: (bi, 5, 0, 0)),
    ]
    args = [ys, ys, wts, x1, gate]
    if final:
        in_specs.append(pl.BlockSpec((1, d), lambda bi, ti: (0, 0)))
        args.append(g_final)
    return pl.pallas_call(
        functools.partial(_moe_combine_kernel, final=final),
        grid=(b, nt),
        in_specs=in_specs,
        out_specs=pl.BlockSpec((None, tm, d), row),
        out_shape=jax.ShapeDtypeStruct((b, t_len, d), F32),
        compiler_params=_cparams(("arbitrary", "arbitrary")),
        name="moe_combine",
    )(*args)


def _moe_small_kernel(*refs, final):
    if final:
        h_ref, idx_ref, wts_ref, x1_ref, gate_ref, wg_ref, wu_ref, wd_ref, gf_ref, o_ref, acc_ref = refs
    else:
        h_ref, idx_ref, wts_ref, x1_ref, gate_ref, wg_ref, wu_ref, wd_ref, o_ref, acc_ref = refs
    e = pl.program_id(0)
    j = pl.program_id(1)

    @pl.when((e == 0) & (j == 0))
    def _():
        acc_ref[...] = jnp.zeros_like(acc_ref)

    hb = h_ref[...].astype(BF16)
    gp = jnp.dot(hb, wg_ref[...].astype(BF16), preferred_element_type=F32)
    up = jnp.dot(hb, wu_ref[...].astype(BF16), preferred_element_type=F32)
    act = (_silu(gp) * up).astype(BF16)
    y = jnp.dot(act, wd_ref[...].astype(BF16), preferred_element_type=F32)
    idx = idx_ref[...]
    wts = wts_ref[...]
    ge = (jnp.where(idx[:, 0:1] == e, wts[:, 0:1], 0.0) + jnp.where(idx[:, 1:2] == e, wts[:, 1:2], 0.0))
    acc_ref[...] += ge * y

    @pl.when((e == pl.num_programs(0) - 1) & (j == pl.num_programs(1) - 1))
    def _():
        x = x1_ref[...] + gate_ref[...] * acc_ref[...]
        o_ref[...] = _rms(x, gf_ref[...]) if final else x


def _moe_small(h2, idx, wts, x1, gate_rows, wg, wu, wd, g_final):
    rws, d = x1.shape
    n_experts, _, dff = wg.shape
    fc = _ff_chunk(dff)
    final = g_final is not None
    full = lambda e, j: (0, 0)
    in_specs = [
        pl.BlockSpec((rws, d), full), pl.BlockSpec((rws, 2), full), pl.BlockSpec((rws, 2), full),
        pl.BlockSpec((rws, d), full),
        pl.BlockSpec((None, rws, d), lambda e, j: (5, 0, 0)),
        pl.BlockSpec((None, d, fc), lambda e, j: (e, 0, j)),
        pl.BlockSpec((None, d, fc), lambda e, j: (e, 0, j)),
        pl.BlockSpec((None, fc, d), lambda e, j: (e, j, 0)),
    ]
    args = [h2, idx, wts, x1, gate_rows, wg, wu, wd]
    if final:
        in_specs.append(pl.BlockSpec((1, d), full))
        args.append(g_final)
    return pl.pallas_call(
        functools.partial(_moe_small_kernel, final=final),
        grid=(n_experts, dff // fc),
        in_specs=in_specs,
        out_specs=pl.BlockSpec((rws, d), full),
        out_shape=jax.ShapeDtypeStruct((rws, d), F32),
        scratch_shapes=[pltpu.VMEM((rws, d), F32)],
        compiler_params=_cparams(("arbitrary", "arbitrary")),
        name="moe_small",
    )(*args)


def _rope_tables(pos, head_dim):
    half = head_dim // 2
    inv_freq = 1.0 / (ROPE_THETA ** (jnp.arange(half, dtype=F32) / half))
    ang = pos.astype(F32)[:, None] * inv_freq[None, :]
    cos = jnp.tile(jnp.cos(ang), (1, LANES // half))
    sin = jnp.tile(jnp.sin(ang), (1, LANES // half))
    lane = jnp.arange(LANES)
    sign = jnp.where((lane % head_dim) < half, -1.0, 1.0).astype(F32)
    return cos, sin * sign[None, :]


def kernel(x_prompt, x_sample, c_prompt, c_sample, cache_k, cache_v, state_conv, w_ada, b_ada, g_mix, g_ffn, w_in, w_dw, b_dw, g_conv_ln, b_conv_ln, lambda_q1, lambda_k1, lambda_q2, lambda_k2, g_subln, w_out, w_ffn_gate, w_ffn_up, w_ffn_down, w_router, w_moe_gate, w_moe_up, w_moe_down, g_final):
    depth, d, _ = w_in.shape
    bp, tp, _ = x_prompt.shape
    bs, ts, _ = x_sample.shape
    past = cache_k.shape[2]
    n_sub, head_dim = cache_k.shape[3], cache_k.shape[4]
    att_dim = n_sub * head_dim
    width, conv_dim = w_dw.shape[1], w_dw.shape[2]
    n_experts = w_router.shape[2]
    assert tp % Q_BLOCK == 0 and d % LANES == 0 and conv_dim % LANES == 0 and att_dim % LANES == 0
    assert 2 * head_dim == LANES and width - 1 <= CONV_HALO and n_experts <= LANES

    mod_all = _adaln(jnp.concatenate([c_prompt, c_sample], axis=0), w_ada, b_ada)
    cos_p, sin_p = _rope_tables(jnp.arange(tp), head_dim)
    cos_s, sin_s = _rope_tables(past + jnp.arange(ts), head_dim)
    hist_p = jnp.zeros((bp, CONV_HALO, conv_dim), F32)
    pad_rows = CONV_HALO - (width - 1)
    cache_k2 = cache_k.reshape(depth, bs, past, att_dim)
    cache_v2 = cache_v.reshape(depth, bs, past, att_dim)
    g_fin = g_final.reshape(1, d)

    xp, xs = x_prompt, x_sample
    outs = {k: [] for k in ("kp", "vp", "cp", "ks", "vs", "cs")}
    for l in range(depth):
        last = l == depth - 1
        moe = l % 2 == 1
        lam_init = 0.8 - 0.6 * math.exp(-0.3 * l)
        w_in_b = w_in[l].astype(BF16)
        wo_b = w_out[l].astype(BF16)
        gm = g_mix[l].reshape(1, d)
        gf = g_ffn[l].reshape(1, d)
        lam_args = [a[l].reshape(1, head_dim) for a in (lambda_q1, lambda_k1, lambda_q2, lambda_k2)]
        gsub = g_subln[l].reshape(1, 2 * head_dim)
        conv_args = (w_dw[l], b_dw[l].reshape(1, conv_dim), g_conv_ln[l].reshape(1, conv_dim),
                     b_conv_ln[l].reshape(1, conv_dim))
        mod_p = mod_all[l, :bp].reshape(bp, 6, 1, d)
        mod_s = mod_all[l, bp:].reshape(bs, 6, 1, d)
        mod_s_rows = jnp.broadcast_to(mod_all[l, bp:].reshape(bs, 1, 6, d), (bs, ts, 6, d))
        mod_s_rows = mod_s_rows.transpose(2, 0, 1, 3).reshape(6, 1, bs * ts, d)
        if moe:
            i = l // 2
            wr_b = jnp.pad(w_router[i], ((0, 0), (0, LANES - n_experts))).astype(BF16)
        else:
            i = l // 2
            wr_b = None
            wg_b, wu_b, wd_b = (w_ffn_gate[i].astype(BF16), w_ffn_up[i].astype(BF16), w_ffn_down[i].astype(BF16))

        kp, vp, cp, qb, kb, vt, cv = _mixer_in(xp, mod_p, gm, w_in_b, cos_p, sin_p, hist_p, *conv_args,
                                               head_dim=head_dim, transpose_v=True)
        att = _attn_prompt(qb, kb, vt, *lam_args, gsub, head_dim=head_dim, lam_init=lam_init)
        res = _out_proj(cv, att, xp, mod_p, mod_p, mod_p, gf, wo_b, wr_b, per_row_mod=False, n_experts=n_experts)
        if moe:
            x1, h2, idx, wts = res
            ys = _moe_experts(h2.reshape(bp * tp, d), idx.reshape(bp * tp, 2),
                              w_moe_gate[i], w_moe_up[i], w_moe_down[i])
            xp = _moe_combine(ys, wts, x1, mod_p, g_fin if last else None)
        else:
            x1, h2 = res
            xp = _dense_ffn(h2, x1, mod_p, wg_b, wu_b, wd_b, g_fin if last else None, per_row_mod=False)
        outs["kp"].append(kp.reshape(bp, tp, n_sub, head_dim))
        outs["vp"].append(vp.reshape(bp, tp, n_sub // 2, 2 * head_dim))
        outs["cp"].append(cp)

        hist_s = jnp.pad(state_conv[l], ((0, 0), (pad_rows, 0), (0, 0)))
        ks, vs, cs, qb, kb, vb, cv = _mixer_in(xs, mod_s, gm, w_in_b, cos_s, sin_s, hist_s, *conv_args,
                                               head_dim=head_dim, transpose_v=False)
        att = _attn_sample(qb, kb, vb, cache_k2[l], cache_v2[l], *lam_args, gsub,
                           head_dim=head_dim, lam_init=lam_init)
        rows = bs * ts
        res = _out_proj(cv.reshape(1, rows, conv_dim), att.reshape(1, rows, att_dim), xs.reshape(1, rows, d),
                        mod_s_rows, mod_s_rows, mod_s_rows, gf, wo_b, wr_b, per_row_mod=True, n_experts=n_experts)
        if moe:
            x1, h2, idx, wts = res
            xs = _moe_small(h2.reshape(rows, d), idx.reshape(rows, 2), wts.reshape(rows, 2), x1.reshape(rows, d),
                            mod_s_rows.reshape(6, rows, d), w_moe_gate[i], w_moe_up[i], w_moe_down[i],
                            g_fin if last else None)
        else:
            x1, h2 = res
            xs = _dense_ffn(h2, x1, mod_s_rows, wg_b, wu_b, wd_b, g_fin if last else None, per_row_mod=True)
        xs = xs.reshape(bs, ts, d)
        outs["ks"].append(ks.reshape(bs, ts, n_sub, head_dim))
        outs["vs"].append(vs.reshape(bs, ts, n_sub // 2, 2 * head_dim))
        outs["cs"].append(cs)

    return (xp, xs, jnp.stack(outs["kp"]), jnp.stack(outs["vp"]), jnp.stack(outs["cp"]),
            jnp.stack(outs["ks"]), jnp.stack(outs["vs"]), jnp.stack(outs["cs"]))
```

```python
import functools
import math

import jax
import jax.numpy as jnp
from jax import lax
from jax.experimental import pallas as pl
from jax.experimental.pallas import tpu as pltpu

F32 = jnp.float32
BF16 = jnp.bfloat16

EPS = 1e-6
CHUNK = 64
ROPE_THETA = 10000.0
NEG_INF = -1e30

LANES = 128
SUBLANES = 8
VMEM_LIMIT_BYTES = 56 * 1024 * 1024

CONV_HALO = 32
Q_BLOCK = 256
ADA_COLS = 1536
REDUCE_ROWS = 64
LOG2E = 1.4426950408889634


def _cparams(sem):
    return pltpu.CompilerParams(dimension_semantics=sem, vmem_limit_bytes=VMEM_LIMIT_BYTES)


def _sigmoid(x):
    return 1.0 / (1.0 + jnp.exp(-x))


def _silu(x):
    return x * _sigmoid(x)


def _rms(x, g):
    return x * lax.rsqrt(jnp.mean(x * x, axis=-1, keepdims=True) + EPS) * g


def _col_reduce(x, op):
    rows, cols = x.shape
    if rows > REDUCE_ROWS and rows % REDUCE_ROWS == 0:
        x = op(x.reshape(rows // REDUCE_ROWS, REDUCE_ROWS, cols), axis=0)
    return op(x, axis=0, keepdims=True)


def _adaln_kernel(c_ref, w_ref, b_ref, o_ref):
    s = _silu(c_ref[...]).astype(BF16)
    o_ref[...] = jnp.dot(s, w_ref[...].astype(BF16), preferred_element_type=F32) + b_ref[...]


def _adaln(c_all, w_ada, b_ada):
    depth, d, n = w_ada.shape
    rows = c_all.shape[0]
    nb = ADA_COLS if n % ADA_COLS == 0 else n
    return pl.pallas_call(
        _adaln_kernel,
        grid=(depth, n // nb),
        in_specs=[
            pl.BlockSpec((rows, d), lambda l, j: (0, 0)),
            pl.BlockSpec((None, d, nb), lambda l, j: (l, 0, j)),
            pl.BlockSpec((None, 1, nb), lambda l, j: (l, 0, j)),
        ],
        out_specs=pl.BlockSpec((None, rows, nb), lambda l, j: (l, 0, j)),
        out_shape=jax.ShapeDtypeStruct((depth, rows, n), F32),
        compiler_params=_cparams(("arbitrary", "arbitrary")),
        name="adaln",
    )(c_all, w_ada, b_ada.reshape(depth, 1, n))


def _mixer_in_kernel(x_ref, shift_ref, scale_ref, g_ref, w_ref, cos_ref, sin_ref, hist_ref,
                     wdw_ref, bdw_ref, gln_ref, bln_ref,
                     k_ref, v_ref, nc_ref, qb_ref, kb_ref, vx_ref, cv_ref,
                     ext_ref, craw_ref, *, tm, conv_dim, att_dim, head_dim, width, att_scale,
                     transpose_v, row_chunk):
    t = pl.program_id(1)
    nt = pl.num_programs(1)

    @pl.when(t == 0)
    def _():
        for s in range(SUBLANES):
            ext_ref[s, 0:CONV_HALO - s, :] = hist_ref[s:CONV_HALO, :]

    x = x_ref[...]
    h = _rms(x, g_ref[...]) * (1.0 + scale_ref[...]) + shift_ref[...]
    z = jnp.dot(h.astype(BF16), w_ref[...], preferred_element_type=F32)

    u = z[:, :conv_dim] * _sigmoid(z[:, conv_dim:2 * conv_dim])
    for s in range(SUBLANES):
        ext_ref[s, CONV_HALO - s:CONV_HALO - s + tm, :] = u
    first_tap = CONV_HALO - (width - 1)

    for r0 in range(0, tm, row_chunk):
        for lb in range(conv_dim // LANES):
            sl = slice(lb * LANES, (lb + 1) * LANES)
            acc = jnp.broadcast_to(bdw_ref[:, sl], (row_chunk, LANES))
            for w in range(width):
                s = (first_tap + w) % SUBLANES
                a0 = r0 + first_tap + w - s
                acc = acc + ext_ref[s, a0:a0 + row_chunk, sl] * wdw_ref[w:w + 1, sl]
            craw_ref[r0:r0 + row_chunk, sl] = acc

    cr = craw_ref[...]
    mu = jnp.mean(cr, axis=-1, keepdims=True)
    dlt = cr - mu
    var = jnp.mean(dlt * dlt, axis=-1, keepdims=True)
    y = dlt * lax.rsqrt(var + EPS) * gln_ref[...] + bln_ref[...]
    cv_ref[...] = _silu(y).astype(BF16)

    @pl.when(t == nt - 1)
    def _():
        nc_ref[...] = ext_ref[0, tm + first_tap:tm + CONV_HALO, :]

    for s in range(SUBLANES):
        tail = ext_ref[s, tm:tm + CONV_HALO - s, :]
        ext_ref[s, 0:CONV_HALO - s, :] = tail

    lane = lax.broadcasted_iota(jnp.int32, (1, LANES), 1)
    first_half = (lane & (head_dim - 1)) < (head_dim // 2)
    cos = cos_ref[...]
    sin = sin_ref[...]

    def rope(xb):
        partner = jnp.where(first_half,
                            pltpu.roll(xb, LANES - head_dim // 2, 1),
                            pltpu.roll(xb, head_dim // 2, 1))
        return xb * cos + partner * sin

    o = 2 * conv_dim
    for lb in range(att_dim // LANES):
        sl = slice(lb * LANES, (lb + 1) * LANES)
        qblk = rope(z[:, o + lb * LANES:o + (lb + 1) * LANES])
        qb_ref[:, sl] = (qblk * att_scale).astype(BF16)
        kblk = rope(z[:, o + att_dim + lb * LANES:o + att_dim + (lb + 1) * LANES])
        k_ref[:, sl] = kblk
        kb_ref[:, sl] = kblk.astype(BF16)
    v = z[:, o + 2 * att_dim:]
    v_ref[...] = v
    if transpose_v:
        vx_ref[...] = v.T.astype(BF16)
    else:
        vx_ref[...] = v.astype(BF16)


def _mixer_in(x, mod, g, w_in_b, cos_t, sin_t, hist, w_dw, b_dw, g_ln, b_ln, *, head_dim, transpose_v):
    b, t_len, d = x.shape
    conv_dim = w_dw.shape[1]
    width = w_dw.shape[0]
    att_dim = (w_in_b.shape[1] - 2 * conv_dim) // 3
    tm = min(512, t_len)
    row_chunk = min(32, tm)
    nt = t_len // tm
    kern = functools.partial(
        _mixer_in_kernel, tm=tm, conv_dim=conv_dim, att_dim=att_dim, head_dim=head_dim, width=width,
        att_scale=head_dim ** -0.5 * LOG2E, transpose_v=transpose_v, row_chunk=row_chunk)
    row = lambda bi, ti: (bi, ti, 0)
    vec = lambda bi, ti: (0, 0)
    if transpose_v:
        vx_spec = pl.BlockSpec((None, att_dim, tm), lambda bi, ti: (bi, 0, ti))
        vx_shape = jax.ShapeDtypeStruct((b, att_dim, t_len), BF16)
    else:
        vx_spec = pl.BlockSpec((None, tm, att_dim), row)
        vx_shape = jax.ShapeDtypeStruct((b, t_len, att_dim), BF16)
    return pl.pallas_call(
        kern,
        grid=(b, nt),
        in_specs=[
            pl.BlockSpec((None, tm, d), row),
            pl.BlockSpec((None, None, 1, d), lambda bi, ti: (bi, 0, 0, 0)),
            pl.BlockSpec((None, None, 1, d), lambda bi, ti: (bi, 1, 0, 0)),
            pl.BlockSpec((1, d), vec),
            pl.BlockSpec(w_in_b.shape, vec),
            pl.BlockSpec((tm, LANES), lambda bi, ti: (ti, 0)),
            pl.BlockSpec((tm, LANES), lambda bi, ti: (ti, 0)),
            pl.BlockSpec((None, CONV_HALO, conv_dim), lambda bi, ti: (bi, 0, 0)),
            pl.BlockSpec((width, conv_dim), vec),
            pl.BlockSpec((1, conv_dim), vec),
            pl.BlockSpec((1, conv_dim), vec),
            pl.BlockSpec((1, conv_dim), vec),
        ],
        out_specs=[
            pl.BlockSpec((None, tm, att_dim), row),
            pl.BlockSpec((None, tm, att_dim), row),
            pl.BlockSpec((None, width - 1, conv_dim), lambda bi, ti: (bi, 0, 0)),
            pl.BlockSpec((None, tm, att_dim), row),
            pl.BlockSpec((None, tm, att_dim), row),
            vx_spec,
            pl.BlockSpec((None, tm, conv_dim), row),
        ],
        out_shape=[
            jax.ShapeDtypeStruct((b, t_len, att_dim), F32),
            jax.ShapeDtypeStruct((b, t_len, att_dim), F32),
            jax.ShapeDtypeStruct((b, width - 1, conv_dim), F32),
            jax.ShapeDtypeStruct((b, t_len, att_dim), BF16),
            jax.ShapeDtypeStruct((b, t_len, att_dim), BF16),
            vx_shape,
            jax.ShapeDtypeStruct((b, t_len, conv_dim), BF16),
        ],
        scratch_shapes=[
            pltpu.VMEM((SUBLANES, CONV_HALO + tm, conv_dim), F32),
            pltpu.VMEM((tm, conv_dim), F32),
        ],
        compiler_params=_cparams(("arbitrary", "arbitrary")),
        name="mixer_in",
    )(x, mod, mod, g, w_in_b, cos_t, sin_t, hist, w_dw, b_dw, g_ln, b_ln)


def _lambda(lq1_ref, lk1_ref, lq2_ref, lk2_ref, lam_init):
    a = jnp.sum(lq1_ref[...] * lk1_ref[...], axis=-1, keepdims=True)
    b = jnp.sum(lq2_ref[...] * lk2_ref[...], axis=-1, keepdims=True)
    return jnp.exp(a) - jnp.exp(b) + lam_init


def _attn_prompt_kernel(q_ref, k_ref, vt_ref, lq1_ref, lk1_ref, lq2_ref, lk2_ref, gsub_ref, o_ref,
                        s_ref, e_ref, *, t_len, head_dim, lam_init):
    lam = _lambda(lq1_ref, lk1_ref, lq2_ref, lk2_ref, lam_init)
    n_pairs = q_ref.shape[-1] // LANES
    lane = lax.broadcasted_iota(jnp.int32, (1, LANES), 1)
    low = lane < head_dim
    kc = lax.broadcasted_iota(jnp.int32, (Q_BLOCK, Q_BLOCK), 0) // CHUNK
    qc = lax.broadcasted_iota(jnp.int32, (Q_BLOCK, Q_BLOCK), 1) // CHUNK
    visible = kc <= qc
    nt_dims = (((1,), (1,)), ((), ()))

    units = [(j, hp, sub) for j in range(t_len // Q_BLOCK) for hp in range(n_pairs) for sub in range(2)]

    def scores(n):
        j, hp, sub = units[n]
        tk = (j + 1) * Q_BLOCK
        sl = slice(hp * LANES, (hp + 1) * LANES)
        qp = q_ref[j * Q_BLOCK:tk, sl]
        qm = jnp.where(low if sub == 0 else jnp.logical_not(low), qp, jnp.zeros_like(qp))
        s = lax.dot_general(k_ref[0:tk, sl], qm, nt_dims, preferred_element_type=F32)
        diag = jnp.where(visible, s[j * Q_BLOCK:, :], NEG_INF)
        s = diag if j == 0 else jnp.concatenate([s[:j * Q_BLOCK, :], diag], axis=0)
        s_ref[n % 2, 0:tk, :] = s
        return _col_reduce(s, jnp.max)

    def probs(n, m):
        tk = (units[n][0] + 1) * Q_BLOCK
        l = None
        for r0 in range(0, tk, Q_BLOCK):
            e = jnp.exp2(s_ref[n % 2, r0:r0 + Q_BLOCK, :] - m)
            e_ref[n % 2, r0:r0 + Q_BLOCK, :] = e.astype(BF16)
            part = jnp.sum(e.reshape(Q_BLOCK // REDUCE_ROWS, REDUCE_ROWS, Q_BLOCK), axis=0)
            l = part if l is None else l + part
        return jnp.sum(l, axis=0, keepdims=True)

    def values(n):
        j, hp, _ = units[n]
        tk = (j + 1) * Q_BLOCK
        sl = slice(hp * LANES, (hp + 1) * LANES)
        return jnp.dot(vt_ref[sl, 0:tk], e_ref[n % 2, 0:tk, :], preferred_element_type=F32)

    maxes, sums, outs = {}, {}, {}
    for step in range(len(units) + 2):
        if step < len(units):
            maxes[step] = scores(step)
        if 0 <= step - 1 < len(units):
            sums[step - 1] = probs(step - 1, maxes.pop(step - 1))
        n = step - 2
        if 0 <= n < len(units):
            outs[n] = values(n)
            j, hp, sub = units[n]
            if sub == 1:
                ot = outs.pop(n - 1) * (1.0 / sums.pop(n - 1)) - outs.pop(n) * (lam / sums.pop(n))
                ms = jnp.mean(ot * ot, axis=0, keepdims=True)
                on = (ot * lax.rsqrt(ms + EPS)).T * (gsub_ref[...] * (1.0 - lam_init))
                o_ref[j * Q_BLOCK:(j + 1) * Q_BLOCK, hp * LANES:(hp + 1) * LANES] = on.astype(BF16)


def _attn_prompt(qb, kb, vt, lq1, lk1, lq2, lk2, gsub, *, head_dim, lam_init):
    b, t_len, a = qb.shape
    kern = functools.partial(_attn_prompt_kernel, t_len=t_len, head_dim=head_dim, lam_init=lam_init)
    vec = lambda bi: (0, 0)
    return pl.pallas_call(
        kern,
        grid=(b,),
        in_specs=[
            pl.BlockSpec((None, t_len, a), lambda bi: (bi, 0, 0)),
            pl.BlockSpec((None, t_len, a), lambda bi: (bi, 0, 0)),
            pl.BlockSpec((None, a, t_len), lambda bi: (bi, 0, 0)),
            pl.BlockSpec((1, head_dim), vec), pl.BlockSpec((1, head_dim), vec),
            pl.BlockSpec((1, head_dim), vec), pl.BlockSpec((1, head_dim), vec),
            pl.BlockSpec((1, 2 * head_dim), vec),
        ],
        out_specs=pl.BlockSpec((None, t_len, a), lambda bi: (bi, 0, 0)),
        out_shape=jax.ShapeDtypeStruct((b, t_len, a), BF16),
        scratch_shapes=[pltpu.VMEM((2, t_len, Q_BLOCK), F32), pltpu.VMEM((2, t_len, Q_BLOCK), BF16)],
        compiler_params=_cparams(("arbitrary",)),
        name="attn_prompt",
    )(qb, kb, vt, lq1, lk1, lq2, lk2, gsub)


def _attn_sample_kernel(q_ref, kn_ref, vn_ref, kc_ref, vc_ref, lq1_ref, lk1_ref, lq2_ref, lk2_ref, gsub_ref,
                        o_ref, *, head_dim, lam_init):
    lam = _lambda(lq1_ref, lk1_ref, lq2_ref, lk2_ref, lam_init)
    n_pairs = q_ref.shape[-1] // LANES
    lane = lax.broadcasted_iota(jnp.int32, (1, LANES), 1)
    low = lane < head_dim
    nt_dims = (((1,), (1,)), ((), ()))
    for hp in range(n_pairs):
        sl = slice(hp * LANES, (hp + 1) * LANES)
        qp = q_ref[:, sl]
        kc = kc_ref[:, sl].astype(BF16)
        vc = vc_ref[:, sl].astype(BF16)
        kn = kn_ref[:, sl]
        vn = vn_ref[:, sl]
        parts = []
        for sub in range(2):
            qm = jnp.where(low if sub == 0 else jnp.logical_not(low), qp, jnp.zeros_like(qp))
            sc = lax.dot_general(qm, kc, nt_dims, preferred_element_type=F32)
            sn = lax.dot_general(qm, kn, nt_dims, preferred_element_type=F32)
            m = jnp.maximum(jnp.max(sc, axis=-1, keepdims=True), jnp.max(sn, axis=-1, keepdims=True))
            ec = jnp.exp2(sc - m)
            en = jnp.exp2(sn - m)
            l = jnp.sum(ec, axis=-1, keepdims=True) + jnp.sum(en, axis=-1, keepdims=True)
            o = (jnp.dot(ec.astype(BF16), vc, preferred_element_type=F32)
                 + jnp.dot(en.astype(BF16), vn, preferred_element_type=F32))
            parts.append((o, l))
        oh = parts[0][0] * (1.0 / parts[0][1]) - parts[1][0] * (lam / parts[1][1])
        on = _rms(oh, gsub_ref[...]) * (1.0 - lam_init)
        o_ref[:, sl] = on.astype(BF16)


def _attn_sample(qb, kb, vb, cache_k, cache_v, lq1, lk1, lq2, lk2, gsub, *, head_dim, lam_init):
    b, tq, a = qb.shape
    past = cache_k.shape[1]
    kern = functools.partial(_attn_sample_kernel, head_dim=head_dim, lam_init=lam_init)
    vec = lambda bi: (0, 0)
    row = lambda bi: (bi, 0, 0)
    return pl.pallas_call(
        kern,
        grid=(b,),
        in_specs=[
            pl.BlockSpec((None, tq, a), row), pl.BlockSpec((None, tq, a), row), pl.BlockSpec((None, tq, a), row),
            pl.BlockSpec((None, past, a), row), pl.BlockSpec((None, past, a), row),
            pl.BlockSpec((1, head_dim), vec), pl.BlockSpec((1, head_dim), vec),
            pl.BlockSpec((1, head_dim), vec), pl.BlockSpec((1, head_dim), vec),
            pl.BlockSpec((1, 2 * head_dim), vec),
        ],
        out_specs=pl.BlockSpec((None, tq, a), row),
        out_shape=jax.ShapeDtypeStruct((b, tq, a), BF16),
        compiler_params=_cparams(("arbitrary",)),
        name="attn_sample",
    )(qb, kb, vb, cache_k, cache_v, lq1, lk1, lq2, lk2, gsub)


def _out_proj_kernel(*refs, moe, n_experts):
    if moe:
        (cv_ref, att_ref, x_ref, gate_ref, shift_ref, scale_ref, g_ref, wo_ref, wr_ref,
         x1_ref, h2_ref, idx_ref, wts_ref) = refs
    else:
        cv_ref, att_ref, x_ref, gate_ref, shift_ref, scale_ref, g_ref, wo_ref, x1_ref, h2_ref = refs
    c = cv_ref.shape[-1]
    mix = (jnp.dot(cv_ref[...], wo_ref[0:c, :], preferred_element_type=F32)
           + jnp.dot(att_ref[...], wo_ref[c:, :], preferred_element_type=F32))
    x1 = x_ref[...] + gate_ref[...] * mix
    x1_ref[...] = x1
    h2 = _rms(x1, g_ref[...]) * (1.0 + scale_ref[...]) + shift_ref[...]
    h2_ref[...] = h2.astype(h2_ref.dtype)
    if moe:
        logits = jnp.dot(h2.astype(BF16), wr_ref[...], preferred_element_type=F32)
        lane = lax.broadcasted_iota(jnp.int32, logits.shape, 1)
        logits = jnp.where(lane < n_experts, logits, -jnp.inf)
        m1 = jnp.max(logits, axis=-1, keepdims=True)
        i1 = jnp.min(jnp.where(logits == m1, lane, LANES), axis=-1, keepdims=True)
        rest = jnp.where(lane == i1, -jnp.inf, logits)
        m2 = jnp.max(rest, axis=-1, keepdims=True)
        i2 = jnp.min(jnp.where(rest == m2, lane, LANES), axis=-1, keepdims=True)
        e2 = jnp.exp(m2 - m1)
        w1 = 1.0 / (1.0 + e2)
        w2 = e2 / (1.0 + e2)
        col = lax.broadcasted_iota(jnp.int32, idx_ref.shape, 1)
        idx_ref[...] = jnp.where(col == 0, i1, i2)
        wts_ref[...] = jnp.where(col == 0, w1, w2)


def _out_proj(cv, att, x, gate, shift, scale, g, wo_b, wr_b, *, per_row_mod, n_experts):
    moe = wr_b is not None
    b, t_len, d = x.shape
    c = cv.shape[-1]
    tm = min(512, t_len)
    nt = t_len // tm
    row = lambda bi, ti: (bi, ti, 0)
    vec = lambda bi, ti: (0, 0)

    def mod_spec(i):
        if per_row_mod:
            return pl.BlockSpec((None, None, tm, d), lambda bi, ti: (i, bi, ti, 0))
        return pl.BlockSpec((None, None, 1, d), lambda bi, ti: (bi, i, 0, 0))

    in_specs = [
        pl.BlockSpec((None, tm, c), row), pl.BlockSpec((None, tm, c), row), pl.BlockSpec((None, tm, d), row),
        mod_spec(2), mod_spec(3), mod_spec(4),
        pl.BlockSpec((1, d), vec), pl.BlockSpec(wo_b.shape, vec),
    ]
    args = [cv, att, x, gate, shift, scale, g, wo_b]
    out_specs = [pl.BlockSpec((None, tm, d), row), pl.BlockSpec((None, tm, d), row)]
    out_shape = [jax.ShapeDtypeStruct((b, t_len, d), F32),
                 jax.ShapeDtypeStruct((b, t_len, d), F32 if moe else BF16)]
    if moe:
        in_specs.append(pl.BlockSpec(wr_b.shape, vec))
        args.append(wr_b)
        out_specs += [pl.BlockSpec((None, tm, 2), row), pl.BlockSpec((None, tm, 2), row)]
        out_shape += [jax.ShapeDtypeStruct((b, t_len, 2), jnp.int32), jax.ShapeDtypeStruct((b, t_len, 2), F32)]
    return pl.pallas_call(
        functools.partial(_out_proj_kernel, moe=moe, n_experts=n_experts),
        grid=(b, nt),
        in_specs=in_specs,
        out_specs=out_specs,
        out_shape=out_shape,
        compiler_params=_cparams(("arbitrary", "arbitrary")),
        name="out_proj_router" if moe else "out_proj",
    )(*args)


def _dense_ffn_kernel(*refs, final):
    if final:
        h_ref, x1_ref, gate_ref, wg_ref, wu_ref, wd_ref, gf_ref, o_ref = refs
    else:
        h_ref, x1_ref, gate_ref, wg_ref, wu_ref, wd_ref, o_ref = refs
    hb = h_ref[...]
    gp = jnp.dot(hb, wg_ref[...], preferred_element_type=F32)
    up = jnp.dot(hb, wu_ref[...], preferred_element_type=F32)
    act = (_silu(gp) * up).astype(BF16)
    f = jnp.dot(act, wd_ref[...], preferred_element_type=F32)
    x = x1_ref[...] + gate_ref[...] * f
    o_ref[...] = _rms(x, gf_ref[...]) if final else x


def _dense_ffn(h2, x1, gate, wg_b, wu_b, wd_b, g_final, *, per_row_mod):
    b, t_len, d = x1.shape
    tm = min(256, t_len)
    nt = t_len // tm
    final = g_final is not None
    row = lambda bi, ti: (bi, ti, 0)
    vec = lambda bi, ti: (0, 0)
    if per_row_mod:
        gate_spec = pl.BlockSpec((None, None, tm, d), lambda bi, ti: (5, bi, ti, 0))
    else:
        gate_spec = pl.BlockSpec((None, None, 1, d), lambda bi, ti: (bi, 5, 0, 0))
    in_specs = [pl.BlockSpec((None, tm, d), row), pl.BlockSpec((None, tm, d), row), gate_spec,
                pl.BlockSpec(wg_b.shape, vec), pl.BlockSpec(wu_b.shape, vec), pl.BlockSpec(wd_b.shape, vec)]
    args = [h2, x1, gate, wg_b, wu_b, wd_b]
    if final:
        in_specs.append(pl.BlockSpec((1, d), vec))
        args.append(g_final)
    return pl.pallas_call(
        functools.partial(_dense_ffn_kernel, final=final),
        grid=(b, nt),
        in_specs=in_specs,
        out_specs=pl.BlockSpec((None, tm, d), row),
        out_shape=jax.ShapeDtypeStruct((b, t_len, d), F32),
        compiler_params=_cparams(("arbitrary", "arbitrary")),
        name="dense_ffn",
    )(*args)


def _moe_expert_kernel(te_ref, tv_ref, src_hbm, dst_hbm, h_hbm, wg_ref, wu_ref, wd_ref, ys_hbm,
                       src_smem, dst_smem, xs_ref, xb_ref, acc_ref, sems, *, tm):
    i = pl.program_id(0)
    j = pl.program_id(1)
    nt = pl.num_programs(0)
    nj = pl.num_programs(1)
    slot = lax.rem(i, 2)
    other = 1 - slot
    valid = tv_ref[i] > 0
    valid_next = jnp.logical_and(i + 1 < nt, tv_ref[jnp.minimum(i + 1, nt - 1)] > 0)
    valid_prev = jnp.logical_and(i > 0, tv_ref[jnp.maximum(i - 1, 0)] > 0)
    d = xb_ref.shape[-1]
    per_step = (tm // nj) // SUBLANES * SUBLANES
    leftover = tm - per_step * nj
    GATHER, SCATTER, IDX = 0, 1, 2

    def load_indices(table_hbm, tile, smem, buf):
        cp = pltpu.make_async_copy(table_hbm.at[pl.ds(tile, 1)], smem.at[pl.ds(buf, 1)], sems.at[IDX, buf])
        cp.start()
        cp.wait()

    def gather_rows(buf, lo, n):
        def body(g, carry):
            grp = lo // SUBLANES + g
            for u in range(SUBLANES):
                tok = src_smem[buf, grp * SUBLANES + u]
                pltpu.make_async_copy(h_hbm.at[lax.shift_right_logical(tok, 3), pl.ds(tok & (SUBLANES - 1), 1)],
                                      xs_ref.at[buf, grp, pl.ds(u, 1)], sems.at[GATHER, buf]).start()
            return carry
        lax.fori_loop(0, n // SUBLANES, body, 0)

    def scatter_rows(buf, lo, n):
        def body(g, carry):
            grp = lo // SUBLANES + g
            for u in range(SUBLANES):
                row = dst_smem[buf, grp * SUBLANES + u]
                pltpu.make_async_copy(acc_ref.at[buf, grp, pl.ds(u, 1)],
                                      ys_hbm.at[lax.shift_right_logical(row, 3), pl.ds(row & (SUBLANES - 1), 1)],
                                      sems.at[SCATTER, buf]).start()
            return carry
        lax.fori_loop(0, n // SUBLANES, body, 0)

    def step_share(fn, buf):
        fn(buf, j * per_step, per_step)
        if leftover:
            @pl.when(j == 0)
            def _():
                fn(buf, nj * per_step, leftover)

    @pl.when(jnp.logical_and(valid, j == 0))
    def _():
        @pl.when(i == 0)
        def _():
            load_indices(src_hbm, 0, src_smem, 0)
            gather_rows(0, 0, tm)

        load_indices(dst_hbm, i, dst_smem, slot)

        @pl.when(valid_next)
        def _():
            load_indices(src_hbm, i + 1, src_smem, other)

        pltpu.make_async_copy(h_hbm.at[pl.ds(0, tm // SUBLANES)], xs_ref.at[slot], sems.at[GATHER, slot]).wait()
        xb_ref[...] = xs_ref[slot].reshape(tm, d).astype(BF16)
        acc_ref[slot] = jnp.zeros((tm // SUBLANES, SUBLANES, d), F32)

        @pl.when(i == 0)
        def _():
            n_dump = ys_hbm.shape[0] - tm // SUBLANES
            fill = pltpu.make_async_copy(acc_ref.at[0], ys_hbm.at[pl.ds(n_dump, tm // SUBLANES)], sems.at[IDX, 0])
            fill.start()
            fill.wait()

    @pl.when(valid_next)
    def _():
        step_share(gather_rows, other)

    @pl.when(valid_prev)
    def _():
        step_share(scatter_rows, other)

    @pl.when(valid)
    def _():
        xb = xb_ref[...]
        gp = jnp.dot(xb, wg_ref[...].astype(BF16), preferred_element_type=F32)
        up = jnp.dot(xb, wu_ref[...].astype(BF16), preferred_element_type=F32)
        act = (_silu(gp) * up).astype(BF16)
        y = jnp.dot(act, wd_ref[...].astype(BF16), preferred_element_type=F32)
        acc_ref[slot] += y.reshape(tm // SUBLANES, SUBLANES, d)

    @pl.when(jnp.logical_and(valid_prev, j == nj - 1))
    def _():
        pltpu.make_async_copy(acc_ref.at[other], ys_hbm.at[pl.ds(0, tm // SUBLANES)],
                              sems.at[SCATTER, other]).wait()


def _moe_tile(n_pairs, n_experts):
    tile = 1024
    while tile > 128 and tile * n_experts > n_pairs:
        tile //= 2
    return tile


def _ff_chunk(dff):
    for c in (512, 256, 128):
        if dff % c == 0:
            return c
    return dff


def _moe_experts(h2, idx, wg, wu, wd):
    n, d = h2.shape
    n_experts, _, dff = wg.shape
    n_pairs = 2 * n
    tm = _moe_tile(n_pairs, n_experts)
    fc = _ff_chunk(dff)
    n_tiles = n_pairs // tm + n_experts
    rows = n_tiles * tm

    e_flat = idx.T.reshape(-1)
    order = jnp.argsort(e_flat, stable=True).astype(jnp.int32)
    counts = jnp.sum((e_flat[:, None] == jnp.arange(n_experts, dtype=jnp.int32)[None, :]).astype(jnp.int32), axis=0)
    padded = ((counts + tm - 1) // tm) * tm
    pad_end = jnp.cumsum(padded)
    pad_start = pad_end - padded
    cnt_start = jnp.cumsum(counts) - counts
    tile_first = jnp.arange(n_tiles, dtype=jnp.int32) * tm
    tile_valid = (tile_first < pad_end[-1]).astype(jnp.int32)
    tile_expert = jnp.minimum(jnp.searchsorted(pad_end, tile_first, side="right"), n_experts - 1).astype(jnp.int32)
    last_valid_expert = tile_expert[jnp.maximum(jnp.sum(tile_valid) - 1, 0)]
    tile_expert = jnp.where(tile_valid > 0, tile_expert, last_valid_expert)
    r = jnp.arange(rows, dtype=jnp.int32)
    e_r = jnp.repeat(tile_expert, tm)
    s = r - pad_start[e_r]
    row_valid = (s < counts[e_r]) & (jnp.repeat(tile_valid, tm) > 0)
    pair = order[jnp.clip(cnt_start[e_r] + s, 0, n_pairs - 1)]
    src = jnp.where(row_valid, pair % n, 0).astype(jnp.int32).reshape(n_tiles, tm)
    dst = jnp.where(row_valid, pair, n_pairs + r % tm).astype(jnp.int32).reshape(n_tiles, tm)

    grid_spec = pltpu.PrefetchScalarGridSpec(
        num_scalar_prefetch=2,
        grid=(n_tiles, dff // fc),
        in_specs=[
            pl.BlockSpec(memory_space=pl.ANY),
            pl.BlockSpec(memory_space=pl.ANY),
            pl.BlockSpec(memory_space=pl.ANY),
            pl.BlockSpec((None, d, fc), lambda i, j, te, tv: (te[i], 0, j)),
            pl.BlockSpec((None, d, fc), lambda i, j, te, tv: (te[i], 0, j)),
            pl.BlockSpec((None, fc, d), lambda i, j, te, tv: (te[i], j, 0)),
        ],
        out_specs=pl.BlockSpec(memory_space=pl.ANY),
        scratch_shapes=[
            pltpu.SMEM((2, tm), jnp.int32),
            pltpu.SMEM((2, tm), jnp.int32),
            pltpu.VMEM((2, tm // SUBLANES, SUBLANES, d), F32),
            pltpu.VMEM((tm, d), BF16),
            pltpu.VMEM((2, tm // SUBLANES, SUBLANES, d), F32),
            pltpu.SemaphoreType.DMA((3, 2)),
        ],
    )
    ys = pl.pallas_call(
        functools.partial(_moe_expert_kernel, tm=tm),
        grid_spec=grid_spec,
        out_shape=jax.ShapeDtypeStruct(((n_pairs + tm) // SUBLANES, SUBLANES, d), F32),
        compiler_params=_cparams(("arbitrary", "arbitrary")),
        name="moe_experts",
    )(tile_expert, tile_valid, src, dst, h2.reshape(n // SUBLANES, SUBLANES, d), wg, wu, wd)
    return ys.reshape(n_pairs + tm, d)


def _moe_combine_kernel(*refs, final):
    if final:
        y0_ref, y1_ref, wts_ref, x1_ref, gate_ref, gf_ref, o_ref = refs
    else:
        y0_ref, y1_ref, wts_ref, x1_ref, gate_ref, o_ref = refs
    w = wts_ref[...]
    y = w[:, 0:1] * y0_ref[...] + w[:, 1:2] * y1_ref[...]
    x = x1_ref[...] + gate_ref[...] * y
    o_ref[...] = _rms(x, gf_ref[...]) if final else x


def _moe_combine(ys, wts, x1, gate, g_final):
    b, t_len, d = x1.shape
    n = b * t_len
    tm = min(512, t_len)
    nt = t_len // tm
    final = g_final is not None
    row = lambda bi, ti: (bi, ti, 0)
    in_specs = [
        pl.BlockSpec((tm, d), lambda bi, ti: (bi * nt + ti, 0)),
        pl.BlockSpec((tm, d), lambda bi, ti: (n // tm + bi * nt + ti, 0)),
        pl.BlockSpec((None, tm, 2), row),
        pl.BlockSpec((None, tm, d), row),
        pl.BlockSpec((None, None, 1, d), lambda bi, ti: (bi, 5, 0, 0)),
    ]
    args = [ys, ys, wts, x1, gate]
    if final:
        in_specs.append(pl.BlockSpec((1, d), lambda bi, ti: (0, 0)))
        args.append(g_final)
    return pl.pallas_call(
        functools.partial(_moe_combine_kernel, final=final),
        grid=(b, nt),
        in_specs=in_specs,
        out_specs=pl.BlockSpec((None, tm, d), row),
        out_shape=jax.ShapeDtypeStruct((b, t_len, d), F32),
        compiler_params=_cparams(("arbitrary", "arbitrary")),
        name="moe_combine",
    )(*args)


def _moe_small_kernel(*refs, final):
    if final:
        h_ref, idx_ref, wts_ref, x1_ref, gate_ref, wg_ref, wu_ref, wd_ref, gf_ref, o_ref, acc_ref = refs
    else:
        h_ref, idx_ref, wts_ref, x1_ref, gate_ref, wg_ref, wu_ref, wd_ref, o_ref, acc_ref = refs
    e = pl.program_id(0)
    j = pl.program_id(1)

    @pl.when((e == 0) & (j == 0))
    def _():
        acc_ref[...] = jnp.zeros_like(acc_ref)

    hb = h_ref[...].astype(BF16)
    gp = jnp.dot(hb, wg_ref[...].astype(BF16), preferred_element_type=F32)
    up = jnp.dot(hb, wu_ref[...].astype(BF16), preferred_element_type=F32)
    act = (_silu(gp) * up).astype(BF16)
    y = jnp.dot(act, wd_ref[...].astype(BF16), preferred_element_type=F32)
    idx = idx_ref[...]
    wts = wts_ref[...]
    ge = (jnp.where(idx[:, 0:1] == e, wts[:, 0:1], 0.0) + jnp.where(idx[:, 1:2] == e, wts[:, 1:2], 0.0))
    acc_ref[...] += ge * y

    @pl.when((e == pl.num_programs(0) - 1) & (j == pl.num_programs(1) - 1))
    def _():
        x = x1_ref[...] + gate_ref[...] * acc_ref[...]
        o_ref[...] = _rms(x, gf_ref[...]) if final else x


def _moe_small(h2, idx, wts, x1, gate_rows, wg, wu, wd, g_final):
    rws, d = x1.shape
    n_experts, _, dff = wg.shape
    fc = _ff_chunk(dff)
    final = g_final is not None
    full = lambda e, j: (0, 0)
    in_specs = [
        pl.BlockSpec((rws, d), full), pl.BlockSpec((rws, 2), full), pl.BlockSpec((rws, 2), full),
        pl.BlockSpec((rws, d), full),
        pl.BlockSpec((None, rws, d), lambda e, j: (5, 0, 0)),
        pl.BlockSpec((None, d, fc), lambda e, j: (e, 0, j)),
        pl.BlockSpec((None, d, fc), lambda e, j: (e, 0, j)),
        pl.BlockSpec((None, fc, d), lambda e, j: (e, j, 0)),
    ]
    args = [h2, idx, wts, x1, gate_rows, wg, wu, wd]
    if final:
        in_specs.append(pl.BlockSpec((1, d), full))
        args.append(g_final)
    return pl.pallas_call(
        functools.partial(_moe_small_kernel, final=final),
        grid=(n_experts, dff // fc),
        in_specs=in_specs,
        out_specs=pl.BlockSpec((rws, d), full),
        out_shape=jax.ShapeDtypeStruct((rws, d), F32),
        scratch_shapes=[pltpu.VMEM((rws, d), F32)],
        compiler_params=_cparams(("arbitrary", "arbitrary")),
        name="moe_small",
    )(*args)


def _rope_tables(pos, head_dim):
    half = head_dim // 2
    inv_freq = 1.0 / (ROPE_THETA ** (jnp.arange(half, dtype=F32) / half))
    ang = pos.astype(F32)[:, None] * inv_freq[None, :]
    cos = jnp.tile(jnp.cos(ang), (1, LANES // half))
    sin = jnp.tile(jnp.sin(ang), (1, LANES // half))
    lane = jnp.arange(LANES)
    sign = jnp.where((lane % head_dim) < half, -1.0, 1.0).astype(F32)
    return cos, sin * sign[None, :]


def kernel(x_prompt, x_sample, c_prompt, c_sample, cache_k, cache_v, state_conv, w_ada, b_ada, g_mix, g_ffn, w_in, w_dw, b_dw, g_conv_ln, b_conv_ln, lambda_q1, lambda_k1, lambda_q2, lambda_k2, g_subln, w_out, w_ffn_gate, w_ffn_up, w_ffn_down, w_router, w_moe_gate, w_moe_up, w_moe_down, g_final):
    depth, d, _ = w_in.shape
    bp, tp, _ = x_prompt.shape
    bs, ts, _ = x_sample.shape
    past = cache_k.shape[2]
    n_sub, head_dim = cache_k.shape[3], cache_k.shape[4]
    att_dim = n_sub * head_dim
    width, conv_dim = w_dw.shape[1], w_dw.shape[2]
    n_experts = w_router.shape[2]
    assert tp % Q_BLOCK == 0 and d % LANES == 0 and conv_dim % LANES == 0 and att_dim % LANES == 0
    assert 2 * head_dim == LANES and width - 1 <= CONV_HALO and n_experts <= LANES

    mod_all = _adaln(jnp.concatenate([c_prompt, c_sample], axis=0), w_ada, b_ada)
    cos_p, sin_p = _rope_tables(jnp.arange(tp), head_dim)
    cos_s, sin_s = _rope_tables(past + jnp.arange(ts), head_dim)
    hist_p = jnp.zeros((bp, CONV_HALO, conv_dim), F32)
    pad_rows = CONV_HALO - (width - 1)
    cache_k2 = cache_k.reshape(depth, bs, past, att_dim)
    cache_v2 = cache_v.reshape(depth, bs, past, att_dim)
    g_fin = g_final.reshape(1, d)

    xp, xs = x_prompt, x_sample
    outs = {k: [] for k in ("kp", "vp", "cp", "ks", "vs", "cs")}
    for l in range(depth):
        last = l == depth - 1
        moe = l % 2 == 1
        lam_init = 0.8 - 0.6 * math.exp(-0.3 * l)
        w_in_b = w_in[l].astype(BF16)
        wo_b = w_out[l].astype(BF16)
        gm = g_mix[l].reshape(1, d)
        gf = g_ffn[l].reshape(1, d)
        lam_args = [a[l].reshape(1, head_dim) for a in (lambda_q1, lambda_k1, lambda_q2, lambda_k2)]
        gsub = g_subln[l].reshape(1, 2 * head_dim)
        conv_args = (w_dw[l], b_dw[l].reshape(1, conv_dim), g_conv_ln[l].reshape(1, conv_dim),
                     b_conv_ln[l].reshape(1, conv_dim))
        mod_p = mod_all[l, :bp].reshape(bp, 6, 1, d)
        mod_s = mod_all[l, bp:].reshape(bs, 6, 1, d)
        mod_s_rows = jnp.broadcast_to(mod_all[l, bp:].reshape(bs, 1, 6, d), (bs, ts, 6, d))
        mod_s_rows = mod_s_rows.transpose(2, 0, 1, 3).reshape(6, 1, bs * ts, d)
        if moe:
            i = l // 2
            wr_b = jnp.pad(w_router[i], ((0, 0), (0, LANES - n_experts))).astype(BF16)
        else:
            i = l // 2
            wr_b = None
            wg_b, wu_b, wd_b = (w_ffn_gate[i].astype(BF16), w_ffn_up[i].astype(BF16), w_ffn_down[i].astype(BF16))

        kp, vp, cp, qb, kb, vt, cv = _mixer_in(xp, mod_p, gm, w_in_b, cos_p, sin_p, hist_p, *conv_args,
                                               head_dim=head_dim, transpose_v=True)
        att = _attn_prompt(qb, kb, vt, *lam_args, gsub, head_dim=head_dim, lam_init=lam_init)
        res = _out_proj(cv, att, xp, mod_p, mod_p, mod_p, gf, wo_b, wr_b, per_row_mod=False, n_experts=n_experts)
        if moe:
            x1, h2, idx, wts = res
            ys = _moe_experts(h2.reshape(bp * tp, d), idx.reshape(bp * tp, 2),
                              w_moe_gate[i], w_moe_up[i], w_moe_down[i])
            xp = _moe_combine(ys, wts, x1, mod_p, g_fin if last else None)
        else:
            x1, h2 = res
            xp = _dense_ffn(h2, x1, mod_p, wg_b, wu_b, wd_b, g_fin if last else None, per_row_mod=False)
        outs["kp"].append(kp.reshape(bp, tp, n_sub, head_dim))
        outs["vp"].append(vp.reshape(bp, tp, n_sub // 2, 2 * head_dim))
        outs["cp"].append(cp)

        hist_s = jnp.pad(state_conv[l], ((0, 0), (pad_rows, 0), (0, 0)))
        ks, vs, cs, qb, kb, vb, cv = _mixer_in(xs, mod_s, gm, w_in_b, cos_s, sin_s, hist_s, *conv_args,
                                               head_dim=head_dim, transpose_v=False)
        att = _attn_sample(qb, kb, vb, cache_k2[l], cache_v2[l], *lam_args, gsub,
                           head_dim=head_dim, lam_init=lam_init)
        rows = bs * ts
        res = _out_proj(cv.reshape(1, rows, conv_dim), att.reshape(1, rows, att_dim), xs.reshape(1, rows, d),
                        mod_s_rows, mod_s_rows, mod_s_rows, gf, wo_b, wr_b, per_row_mod=True, n_experts=n_experts)
        if moe:
            x1, h2, idx, wts = res
            xs = _moe_small(h2.reshape(rows, d), idx.reshape(rows, 2), wts.reshape(rows, 2), x1.reshape(rows, d),
                            mod_s_rows.reshape(6, rows, d), w_moe_gate[i], w_moe_up[i], w_moe_down[i],
                            g_fin if last else None)
        else:
            x1, h2 = res
            xs = _dense_ffn(h2, x1, mod_s_rows, wg_b, wu_b, wd_b, g_fin if last else None, per_row_mod=True)
        xs = xs.reshape(bs, ts, d)
        outs["ks"].append(ks.reshape(bs, ts, n_sub, head_dim))
        outs["vs"].append(vs.reshape(bs, ts, n_sub // 2, 2 * head_dim))
        outs["cs"].append(cs)

    return (xp, xs, jnp.stack(outs["kp"]), jnp.stack(outs["vp"]), jnp.stack(outs["cp"]),
            jnp.stack(outs["ks"]), jnp.stack(outs["vs"]), jnp.stack(outs["cs"]))
```

```python
import functools
import math

import jax
import jax.numpy as jnp
from jax import lax
from jax.experimental import pallas as pl
from jax.experimental.pallas import tpu as pltpu

F32 = jnp.float32
BF16 = jnp.bfloat16

EPS = 1e-6
CHUNK = 64
ROPE_THETA = 10000.0
NEG_INF = -1e30

LANES = 128
SUBLANES = 8
VMEM_LIMIT_BYTES = 56 * 1024 * 1024

CONV_HALO = 32
Q_BLOCK = 256
ADA_COLS = 1536
REDUCE_ROWS = 64
LOG2E = 1.4426950408889634


def _cparams(sem):
    return pltpu.CompilerParams(dimension_semantics=sem, vmem_limit_bytes=VMEM_LIMIT_BYTES)


def _sigmoid(x):
    return 1.0 / (1.0 + jnp.exp(-x))


def _silu(x):
    return x * _sigmoid(x)


def _rms(x, g):
    return x * lax.rsqrt(jnp.mean(x * x, axis=-1, keepdims=True) + EPS) * g


def _col_reduce(x, op):
    rows, cols = x.shape
    if rows > REDUCE_ROWS and rows % REDUCE_ROWS == 0:
        x = op(x.reshape(rows // REDUCE_ROWS, REDUCE_ROWS, cols), axis=0)
    return op(x, axis=0, keepdims=True)


def _adaln_kernel(c_ref, w_ref, b_ref, o_ref):
    s = _silu(c_ref[...]).astype(BF16)
    o_ref[...] = jnp.dot(s, w_ref[...].astype(BF16), preferred_element_type=F32) + b_ref[...]


def _adaln(c_all, w_ada, b_ada):
    depth, d, n = w_ada.shape
    rows = c_all.shape[0]
    nb = ADA_COLS if n % ADA_COLS == 0 else n
    return pl.pallas_call(
        _adaln_kernel,
        grid=(depth, n // nb),
        in_specs=[
            pl.BlockSpec((rows, d), lambda l, j: (0, 0)),
            pl.BlockSpec((None, d, nb), lambda l, j: (l, 0, j)),
            pl.BlockSpec((None, 1, nb), lambda l, j: (l, 0, j)),
        ],
        out_specs=pl.BlockSpec((None, rows, nb), lambda l, j: (l, 0, j)),
        out_shape=jax.ShapeDtypeStruct((depth, rows, n), F32),
        compiler_params=_cparams(("arbitrary", "arbitrary")),
        name="adaln",
    )(c_all, w_ada, b_ada.reshape(depth, 1, n))


def _mixer_in_kernel(x_ref, shift_ref, scale_ref, g_ref, w_ref, cos_ref, sin_ref, hist_ref,
                     wdw_ref, bdw_ref, gln_ref, bln_ref,
                     k_ref, v_ref, nc_ref, qb_ref, kb_ref, vx_ref, cv_ref,
                     ext_ref, craw_ref, *, tm, conv_dim, att_dim, head_dim, width, att_scale,
                     transpose_v, row_chunk):
    t = pl.program_id(1)
    nt = pl.num_programs(1)

    @pl.when(t == 0)
    def _():
        for s in range(SUBLANES):
            ext_ref[s, 0:CONV_HALO - s, :] = hist_ref[s:CONV_HALO, :]

    x = x_ref[...]
    h = _rms(x, g_ref[...]) * (1.0 + scale_ref[...]) + shift_ref[...]
    z = jnp.dot(h.astype(BF16), w_ref[...], preferred_element_type=F32)

    u = z[:, :conv_dim] * _sigmoid(z[:, conv_dim:2 * conv_dim])
    for s in range(SUBLANES):
        ext_ref[s, CONV_HALO - s:CONV_HALO - s + tm, :] = u
    first_tap = CONV_HALO - (width - 1)

    for r0 in range(0, tm, row_chunk):
        for lb in range(conv_dim // LANES):
            sl = slice(lb * LANES, (lb + 1) * LANES)
            acc = jnp.broadcast_to(bdw_ref[:, sl], (row_chunk, LANES))
            for w in range(width):
                s = (first_tap + w) % SUBLANES
                a0 = r0 + first_tap + w - s
                acc = acc + ext_ref[s, a0:a0 + row_chunk, sl] * wdw_ref[w:w + 1, sl]
            craw_ref[r0:r0 + row_chunk, sl] = acc

    cr = craw_ref[...]
    mu = jnp.mean(cr, axis=-1, keepdims=True)
    dlt = cr - mu
    var = jnp.mean(dlt * dlt, axis=-1, keepdims=True)
    y = dlt * lax.rsqrt(var + EPS) * gln_ref[...] + bln_ref[...]
    cv_ref[...] = _silu(y).astype(BF16)

    @pl.when(t == nt - 1)
    def _():
        nc_ref[...] = ext_ref[0, tm + first_tap:tm + CONV_HALO, :]

    for s in range(SUBLANES):
        tail = ext_ref[s, tm:tm + CONV_HALO - s, :]
        ext_ref[s, 0:CONV_HALO - s, :] = tail

    lane = lax.broadcasted_iota(jnp.int32, (1, LANES), 1)
    first_half = (lane & (head_dim - 1)) < (head_dim // 2)
    cos = cos_ref[...]
    sin = sin_ref[...]

    def rope(xb):
        partner = jnp.where(first_half,
                            pltpu.roll(xb, LANES - head_dim // 2, 1),
                            pltpu.roll(xb, head_dim // 2, 1))
        return xb * cos + partner * sin

    o = 2 * conv_dim
    for lb in range(att_dim // LANES):
        sl = slice(lb * LANES, (lb + 1) * LANES)
        qblk = rope(z[:, o + lb * LANES:o + (lb + 1) * LANES])
        qb_ref[:, sl] = (qblk * att_scale).astype(BF16)
        kblk = rope(z[:, o + att_dim + lb * LANES:o + att_dim + (lb + 1) * LANES])
        k_ref[:, sl] = kblk
        kb_ref[:, sl] = kblk.astype(BF16)
    v = z[:, o + 2 * att_dim:]
    v_ref[...] = v
    if transpose_v:
        vx_ref[...] = v.T.astype(BF16)
    else:
        vx_ref[...] = v.astype(BF16)


def _mixer_in(x, mod, g, w_in_b, cos_t, sin_t, hist, w_dw, b_dw, g_ln, b_ln, *, head_dim, transpose_v):
    b, t_len, d = x.shape
    conv_dim = w_dw.shape[1]
    width = w_dw.shape[0]
    att_dim = (w_in_b.shape[1] - 2 * conv_dim) // 3
    tm = min(512, t_len)
    row_chunk = min(32, tm)
    nt = t_len // tm
    kern = functools.partial(
        _mixer_in_kernel, tm=tm, conv_dim=conv_dim, att_dim=att_dim, head_dim=head_dim, width=width,
        att_scale=head_dim ** -0.5 * LOG2E, transpose_v=transpose_v, row_chunk=row_chunk)
    row = lambda bi, ti: (bi, ti, 0)
    vec = lambda bi, ti: (0, 0)
    if transpose_v:
        vx_spec = pl.BlockSpec((None, att_dim, tm), lambda bi, ti: (bi, 0, ti))
        vx_shape = jax.ShapeDtypeStruct((b, att_dim, t_len), BF16)
    else:
        vx_spec = pl.BlockSpec((None, tm, att_dim), row)
        vx_shape = jax.ShapeDtypeStruct((b, t_len, att_dim), BF16)
    return pl.pallas_call(
        kern,
        grid=(b, nt),
        in_specs=[
            pl.BlockSpec((None, tm, d), row),
            pl.BlockSpec((None, None, 1, d), lambda bi, ti: (bi, 0, 0, 0)),
            pl.BlockSpec((None, None, 1, d), lambda bi, ti: (bi, 1, 0, 0)),
            pl.BlockSpec((1, d), vec),
            pl.BlockSpec(w_in_b.shape, vec),
            pl.BlockSpec((tm, LANES), lambda bi, ti: (ti, 0)),
            pl.BlockSpec((tm, LANES), lambda bi, ti: (ti, 0)),
            pl.BlockSpec((None, CONV_HALO, conv_dim), lambda bi, ti: (bi, 0, 0)),
            pl.BlockSpec((width, conv_dim), vec),
            pl.BlockSpec((1, conv_dim), vec),
            pl.BlockSpec((1, conv_dim), vec),
            pl.BlockSpec((1, conv_dim), vec),
        ],
        out_specs=[
            pl.BlockSpec((None, tm, att_dim), row),
            pl.BlockSpec((None, tm, att_dim), row),
            pl.BlockSpec((None, width - 1, conv_dim), lambda bi, ti: (bi, 0, 0)),
            pl.BlockSpec((None, tm, att_dim), row),
            pl.BlockSpec((None, tm, att_dim), row),
            vx_spec,
            pl.BlockSpec((None, tm, conv_dim), row),
        ],
        out_shape=[
            jax.ShapeDtypeStruct((b, t_len, att_dim), F32),
            jax.ShapeDtypeStruct((b, t_len, att_dim), F32),
            jax.ShapeDtypeStruct((b, width - 1, conv_dim), F32),
            jax.ShapeDtypeStruct((b, t_len, att_dim), BF16),
            jax.ShapeDtypeStruct((b, t_len, att_dim), BF16),
            vx_shape,
            jax.ShapeDtypeStruct((b, t_len, conv_dim), BF16),
        ],
        scratch_shapes=[
            pltpu.VMEM((SUBLANES, CONV_HALO + tm, conv_dim), F32),
            pltpu.VMEM((tm, conv_dim), F32),
        ],
        compiler_params=_cparams(("arbitrary", "arbitrary")),
        name="mixer_in",
    )(x, mod, mod, g, w_in_b, cos_t, sin_t, hist, w_dw, b_dw, g_ln, b_ln)


def _lambda(lq1_ref, lk1_ref, lq2_ref, lk2_ref, lam_init):
    a = jnp.sum(lq1_ref[...] * lk1_ref[...], axis=-1, keepdims=True)
    b = jnp.sum(lq2_ref[...] * lk2_ref[...], axis=-1, keepdims=True)
    return jnp.exp(a) - jnp.exp(b) + lam_init


def _attn_prompt_kernel(q_ref, k_ref, vt_ref, lq1_ref, lk1_ref, lq2_ref, lk2_ref, gsub_ref, o_ref,
                        s_ref, e_ref, *, t_len, head_dim, lam_init):
    lam = _lambda(lq1_ref, lk1_ref, lq2_ref, lk2_ref, lam_init)
    n_pairs = q_ref.shape[-1] // LANES
    lane = lax.broadcasted_iota(jnp.int32, (1, LANES), 1)
    low = lane < head_dim
    kc = lax.broadcasted_iota(jnp.int32, (Q_BLOCK, Q_BLOCK), 0) // CHUNK
    qc = lax.broadcasted_iota(jnp.int32, (Q_BLOCK, Q_BLOCK), 1) // CHUNK
    visible = kc <= qc
    nt_dims = (((1,), (1,)), ((), ()))

    units = [(j, hp, sub) for j in range(t_len // Q_BLOCK) for hp in range(n_pairs) for sub in range(2)]

    def scores(n):
        j, hp, sub = units[n]
        tk = (j + 1) * Q_BLOCK
        sl = slice(hp * LANES, (hp + 1) * LANES)
        qp = q_ref[j * Q_BLOCK:tk, sl]
        qm = jnp.where(low if sub == 0 else jnp.logical_not(low), qp, jnp.zeros_like(qp))
        s = lax.dot_general(k_ref[0:tk, sl], qm, nt_dims, preferred_element_type=F32)
        diag = jnp.where(visible, s[j * Q_BLOCK:, :], NEG_INF)
        s = diag if j == 0 else jnp.concatenate([s[:j * Q_BLOCK, :], diag], axis=0)
        s_ref[n % 2, 0:tk, :] = s
        return _col_reduce(s, jnp.max)

    def probs(n, m):
        tk = (units[n][0] + 1) * Q_BLOCK
        l = None
        for r0 in range(0, tk, Q_BLOCK):
            e = jnp.exp2(s_ref[n % 2, r0:r0 + Q_BLOCK, :] - m)
            e_ref[n % 2, r0:r0 + Q_BLOCK, :] = e.astype(BF16)
            part = jnp.sum(e.reshape(Q_BLOCK // REDUCE_ROWS, REDUCE_ROWS, Q_BLOCK), axis=0)
            l = part if l is None else l + part
        return jnp.sum(l, axis=0, keepdims=True)

    def values(n):
        j, hp, _ = units[n]
        tk = (j + 1) * Q_BLOCK
        sl = slice(hp * LANES, (hp + 1) * LANES)
        return jnp.dot(vt_ref[sl, 0:tk], e_ref[n % 2, 0:tk, :], preferred_element_type=F32)

    maxes, sums, outs = {}, {}, {}
    for step in range(len(units) + 2):
        if step < len(units):
            maxes[step] = scores(step)
        if 0 <= step - 1 < len(units):
            sums[step - 1] = probs(step - 1, maxes.pop(step - 1))
        n = step - 2
        if 0 <= n < len(units):
            outs[n] = values(n)
            j, hp, sub = units[n]
            if sub == 1:
                ot = outs.pop(n - 1) * (1.0 / sums.pop(n - 1)) - outs.pop(n) * (lam / sums.pop(n))
                ms = jnp.mean(ot * ot, axis=0, keepdims=True)
                on = (ot * lax.rsqrt(ms + EPS)).T * (gsub_ref[...] * (1.0 - lam_init))
                o_ref[j * Q_BLOCK:(j + 1) * Q_BLOCK, hp * LANES:(hp + 1) * LANES] = on.astype(BF16)


def _attn_prompt(qb, kb, vt, lq1, lk1, lq2, lk2, gsub, *, head_dim, lam_init):
    b, t_len, a = qb.shape
    kern = functools.partial(_attn_prompt_kernel, t_len=t_len, head_dim=head_dim, lam_init=lam_init)
    vec = lambda bi: (0, 0)
    return pl.pallas_call(
        kern,
        grid=(b,),
        in_specs=[
            pl.BlockSpec((None, t_len, a), lambda bi: (bi, 0, 0)),
            pl.BlockSpec((None, t_len, a), lambda bi: (bi, 0, 0)),
            pl.BlockSpec((None, a, t_len), lambda bi: (bi, 0, 0)),
            pl.BlockSpec((1, head_dim), vec), pl.BlockSpec((1, head_dim), vec),
            pl.BlockSpec((1, head_dim), vec), pl.BlockSpec((1, head_dim), vec),
            pl.BlockSpec((1, 2 * head_dim), vec),
        ],
        out_specs=pl.BlockSpec((None, t_len, a), lambda bi: (bi, 0, 0)),
        out_shape=jax.ShapeDtypeStruct((b, t_len, a), BF16),
        scratch_shapes=[pltpu.VMEM((2, t_len, Q_BLOCK), F32), pltpu.VMEM((2, t_len, Q_BLOCK), BF16)],
        compiler_params=_cparams(("arbitrary",)),
        name="attn_prompt",
    )(qb, kb, vt, lq1, lk1, lq2, lk2, gsub)


def _attn_sample_kernel(q_ref, kn_ref, vn_ref, kc_ref, vc_ref, lq1_ref, lk1_ref, lq2_ref, lk2_ref, gsub_ref,
                        o_ref, *, head_dim, lam_init):
    lam = _lambda(lq1_ref, lk1_ref, lq2_ref, lk2_ref, lam_init)
    n_pairs = q_ref.shape[-1] // LANES
    lane = lax.broadcasted_iota(jnp.int32, (1, LANES), 1)
    low = lane < head_dim
    nt_dims = (((1,), (1,)), ((), ()))
    for hp in range(n_pairs):
        sl = slice(hp * LANES, (hp + 1) * LANES)
        qp = q_ref[:, sl]
        kc = kc_ref[:, sl].astype(BF16)
        vc = vc_ref[:, sl].astype(BF16)
        kn = kn_ref[:, sl]
        vn = vn_ref[:, sl]
        parts = []
        for sub in range(2):
            qm = jnp.where(low if sub == 0 else jnp.logical_not(low), qp, jnp.zeros_like(qp))
            sc = lax.dot_general(qm, kc, nt_dims, preferred_element_type=F32)
            sn = lax.dot_general(qm, kn, nt_dims, preferred_element_type=F32)
            m = jnp.maximum(jnp.max(sc, axis=-1, keepdims=True), jnp.max(sn, axis=-1, keepdims=True))
            ec = jnp.exp2(sc - m)
            en = jnp.exp2(sn - m)
            l = jnp.sum(ec, axis=-1, keepdims=True) + jnp.sum(en, axis=-1, keepdims=True)
            o = (jnp.dot(ec.astype(BF16), vc, preferred_element_type=F32)
                 + jnp.dot(en.astype(BF16), vn, preferred_element_type=F32))
            parts.append((o, l))
        oh = parts[0][0] * (1.0 / parts[0][1]) - parts[1][0] * (lam / parts[1][1])
        on = _rms(oh, gsub_ref[...]) * (1.0 - lam_init)
        o_ref[:, sl] = on.astype(BF16)


def _attn_sample(qb, kb, vb, cache_k, cache_v, lq1, lk1, lq2, lk2, gsub, *, head_dim, lam_init):
    b, tq, a = qb.shape
    past = cache_k.shape[1]
    kern = functools.partial(_attn_sample_kernel, head_dim=head_dim, lam_init=lam_init)
    vec = lambda bi: (0, 0)
    row = lambda bi: (bi, 0, 0)
    return pl.pallas_call(
        kern,
        grid=(b,),
        in_specs=[
            pl.BlockSpec((None, tq, a), row), pl.BlockSpec((None, tq, a), row), pl.BlockSpec((None, tq, a), row),
            pl.BlockSpec((None, past, a), row), pl.BlockSpec((None, past, a), row),
            pl.BlockSpec((1, head_dim), vec), pl.BlockSpec((1, head_dim), vec),
            pl.BlockSpec((1, head_dim), vec), pl.BlockSpec((1, head_dim), vec),
            pl.BlockSpec((1, 2 * head_dim), vec),
        ],
        out_specs=pl.BlockSpec((None, tq, a), row),
        out_shape=jax.ShapeDtypeStruct((b, tq, a), BF16),
        compiler_params=_cparams(("arbitrary",)),
        name="attn_sample",
    )(qb, kb, vb, cache_k, cache_v, lq1, lk1, lq2, lk2, gsub)


def _out_proj_kernel(*refs, moe, n_experts):
    if moe:
        (cv_ref, att_ref, x_ref, gate_ref, shift_ref, scale_ref, g_ref, wo_ref, wr_ref,
         x1_ref, h2_ref, idx_ref, wts_ref) = refs
    else:
        cv_ref, att_ref, x_ref, gate_ref, shift_ref, scale_ref, g_ref, wo_ref, x1_ref, h2_ref = refs
    c = cv_ref.shape[-1]
    mix = (jnp.dot(cv_ref[...], wo_ref[0:c, :], preferred_element_type=F32)
           + jnp.dot(att_ref[...], wo_ref[c:, :], preferred_element_type=F32))
    x1 = x_ref[...] + gate_ref[...] * mix
    x1_ref[...] = x1
    h2 = _rms(x1, g_ref[...]) * (1.0 + scale_ref[...]) + shift_ref[...]
    h2_ref[...] = h2.astype(h2_ref.dtype)
    if moe:
        logits = jnp.dot(h2.astype(BF16), wr_ref[...], preferred_element_type=F32)
        lane = lax.broadcasted_iota(jnp.int32, logits.shape, 1)
        logits = jnp.where(lane < n_experts, logits, -jnp.inf)
        m1 = jnp.max(logits, axis=-1, keepdims=True)
        i1 = jnp.min(jnp.where(logits == m1, lane, LANES), axis=-1, keepdims=True)
        rest = jnp.where(lane == i1, -jnp.inf, logits)
        m2 = jnp.max(rest, axis=-1, keepdims=True)
        i2 = jnp.min(jnp.where(rest == m2, lane, LANES), axis=-1, keepdims=True)
        e2 = jnp.exp(m2 - m1)
        w1 = 1.0 / (1.0 + e2)
        w2 = e2 / (1.0 + e2)
        col = lax.broadcasted_iota(jnp.int32, idx_ref.shape, 1)
        idx_ref[...] = jnp.where(col == 0, i1, i2)
        wts_ref[...] = jnp.where(col == 0, w1, w2)


def _out_proj(cv, att, x, gate, shift, scale, g, wo_b, wr_b, *, per_row_mod, n_experts):
    moe = wr_b is not None
    b, t_len, d = x.shape
    c = cv.shape[-1]
    tm = min(512, t_len)
    nt = t_len // tm
    row = lambda bi, ti: (bi, ti, 0)
    vec = lambda bi, ti: (0, 0)

    def mod_spec(i):
        if per_row_mod:
            return pl.BlockSpec((None, None, tm, d), lambda bi, ti: (i, bi, ti, 0))
        return pl.BlockSpec((None, None, 1, d), lambda bi, ti: (bi, i, 0, 0))

    in_specs = [
        pl.BlockSpec((None, tm, c), row), pl.BlockSpec((None, tm, c), row), pl.BlockSpec((None, tm, d), row),
        mod_spec(2), mod_spec(3), mod_spec(4),
        pl.BlockSpec((1, d), vec), pl.BlockSpec(wo_b.shape, vec),
    ]
    args = [cv, att, x, gate, shift, scale, g, wo_b]
    out_specs = [pl.BlockSpec((None, tm, d), row), pl.BlockSpec((None, tm, d), row)]
    out_shape = [jax.ShapeDtypeStruct((b, t_len, d), F32),
                 jax.ShapeDtypeStruct((b, t_len, d), BF16)]
    if moe:
        in_specs.append(pl.BlockSpec(wr_b.shape, vec))
        args.append(wr_b)
        out_specs += [pl.BlockSpec((None, tm, 2), row), pl.BlockSpec((None, tm, 2), row)]
        out_shape += [jax.ShapeDtypeStruct((b, t_len, 2), jnp.int32), jax.ShapeDtypeStruct((b, t_len, 2), F32)]
    return pl.pallas_call(
        functools.partial(_out_proj_kernel, moe=moe, n_experts=n_experts),
        grid=(b, nt),
        in_specs=in_specs,
        out_specs=out_specs,
        out_shape=out_shape,
        compiler_params=_cparams(("arbitrary", "arbitrary")),
        name="out_proj_router" if moe else "out_proj",
    )(*args)


def _dense_ffn_kernel(*refs, final):
    if final:
        h_ref, x1_ref, gate_ref, wg_ref, wu_ref, wd_ref, gf_ref, o_ref = refs
    else:
        h_ref, x1_ref, gate_ref, wg_ref, wu_ref, wd_ref, o_ref = refs
    hb = h_ref[...]
    gp = jnp.dot(hb, wg_ref[...], preferred_element_type=F32)
    up = jnp.dot(hb, wu_ref[...], preferred_element_type=F32)
    act = (_silu(gp) * up).astype(BF16)
    f = jnp.dot(act, wd_ref[...], preferred_element_type=F32)
    x = x1_ref[...] + gate_ref[...] * f
    o_ref[...] = _rms(x, gf_ref[...]) if final else x


def _dense_ffn(h2, x1, gate, wg_b, wu_b, wd_b, g_final, *, per_row_mod):
    b, t_len, d = x1.shape
    tm = min(256, t_len)
    nt = t_len // tm
    final = g_final is not None
    row = lambda bi, ti: (bi, ti, 0)
    vec = lambda bi, ti: (0, 0)
    if per_row_mod:
        gate_spec = pl.BlockSpec((None, None, tm, d), lambda bi, ti: (5, bi, ti, 0))
    else:
        gate_spec = pl.BlockSpec((None, None, 1, d), lambda bi, ti: (bi, 5, 0, 0))
    in_specs = [pl.BlockSpec((None, tm, d), row), pl.BlockSpec((None, tm, d), row), gate_spec,
                pl.BlockSpec(wg_b.shape, vec), pl.BlockSpec(wu_b.shape, vec), pl.BlockSpec(wd_b.shape, vec)]
    args = [h2, x1, gate, wg_b, wu_b, wd_b]
    if final:
        in_specs.append(pl.BlockSpec((1, d), vec))
        args.append(g_final)
    return pl.pallas_call(
        functools.partial(_dense_ffn_kernel, final=final),
        grid=(b, nt),
        in_specs=in_specs,
        out_specs=pl.BlockSpec((None, tm, d), row),
        out_shape=jax.ShapeDtypeStruct((b, t_len, d), F32),
        compiler_params=_cparams(("arbitrary", "arbitrary")),
        name="dense_ffn",
    )(*args)


def _moe_expert_kernel(te_ref, tv_ref, src_hbm, dst_hbm, h_hbm, wg_ref, wu_ref, wd_ref, ys_hbm,
                       src_smem, dst_smem, xs_ref, xb_ref, acc_ref, sems, *, tm):
    i = pl.program_id(0)
    j = pl.program_id(1)
    nt = pl.num_programs(0)
    nj = pl.num_programs(1)
    parity = lax.rem(i, 2)
    valid = tv_ref[i] > 0
    valid_next = jnp.logical_and(i + 1 < nt, tv_ref[jnp.minimum(i + 1, nt - 1)] > 0)
    valid_prev = jnp.logical_and(i > 0, tv_ref[jnp.maximum(i - 1, 0)] > 0)
    d = xb_ref.shape[-1]
    per_step = (tm // nj) // SUBLANES * SUBLANES
    leftover = tm - per_step * nj
    GATHER, SCATTER, IDX = 0, 1, 2

    def load_indices(table_hbm, tile, smem, buf):
        cp = pltpu.make_async_copy(table_hbm.at[pl.ds(tile, 1)], smem.at[pl.ds(buf, 1)], sems.at[IDX, buf])
        cp.start()
        cp.wait()

    def gather_rows(buf, g0, n):
        def body(g, carry):
            grp = g0 + g
            for u in range(SUBLANES):
                tok = src_smem[buf, grp * SUBLANES + u]
                pltpu.make_async_copy(h_hbm.at[lax.shift_right_logical(tok, 3), pl.ds(tok & (SUBLANES - 1), 1)],
                                      xs_ref.at[buf, grp, pl.ds(u, 1)], sems.at[GATHER, buf]).start()
            return carry
        lax.fori_loop(0, n // SUBLANES, body, 0)

    def scatter_rows(buf, g0, n):
        def body(g, carry):
            grp = g0 + g
            for u in range(SUBLANES):
                row = dst_smem[buf, grp * SUBLANES + u]
                pltpu.make_async_copy(acc_ref.at[buf, grp, pl.ds(u, 1)],
                                      ys_hbm.at[lax.shift_right_logical(row, 3), pl.ds(row & (SUBLANES - 1), 1)],
                                      sems.at[SCATTER, buf]).start()
            return carry
        lax.fori_loop(0, n // SUBLANES, body, 0)

    def step_share(fn, buf):
        fn(buf, j * (per_step // SUBLANES), per_step)
        if leftover:
            @pl.when(j == 0)
            def _():
                fn(buf, nj * per_step // SUBLANES, leftover)

    def run(slot, other):
        @pl.when(jnp.logical_and(valid, j == 0))
        def _():
            if slot == 0:
                @pl.when(i == 0)
                def _():
                    load_indices(src_hbm, 0, src_smem, 0)
                    gather_rows(0, 0, tm)

            load_indices(dst_hbm, i, dst_smem, slot)

            @pl.when(valid_next)
            def _():
                load_indices(src_hbm, i + 1, src_smem, other)

            pltpu.make_async_copy(h_hbm.at[pl.ds(0, tm // SUBLANES)], xs_ref.at[slot],
                                  sems.at[GATHER, slot]).wait()
            xb_ref[...] = xs_ref[slot].reshape(tm, d).astype(BF16)
            acc_ref[slot] = jnp.zeros((tm // SUBLANES, SUBLANES, d), F32)

            if slot == 0:
                @pl.when(i == 0)
                def _():
                    n_dump = ys_hbm.shape[0] - tm // SUBLANES
                    fill = pltpu.make_async_copy(acc_ref.at[0], ys_hbm.at[pl.ds(n_dump, tm // SUBLANES)],
                                                 sems.at[IDX, 0])
                    fill.start()
                    fill.wait()

        @pl.when(valid_next)
        def _():
            step_share(gather_rows, other)

        @pl.when(valid_prev)
        def _():
            step_share(scatter_rows, other)

        @pl.when(valid)
        def _():
            xb = xb_ref[...]
            gp = jnp.dot(xb, wg_ref[...].astype(BF16), preferred_element_type=F32)
            up = jnp.dot(xb, wu_ref[...].astype(BF16), preferred_element_type=F32)
            act = (_silu(gp) * up).astype(BF16)
            y = jnp.dot(act, wd_ref[...].astype(BF16), preferred_element_type=F32)
            acc_ref[slot] += y.reshape(tm // SUBLANES, SUBLANES, d)

        @pl.when(jnp.logical_and(valid_prev, j == nj - 1))
        def _():
            pltpu.make_async_copy(acc_ref.at[other], ys_hbm.at[pl.ds(0, tm // SUBLANES)],
                                  sems.at[SCATTER, other]).wait()

    for par in (0, 1):
        @pl.when(parity == par)
        def _(par=par):
            run(par, 1 - par)


def _moe_tile(n_pairs, n_experts):
    tile = 1024
    while tile > 128 and tile * n_experts > n_pairs:
        tile //= 2
    return tile


def _ff_chunk(dff):
    for c in (512, 256, 128):
        if dff % c == 0:
            return c
    return dff


def _moe_experts(h2, idx, wg, wu, wd):
    n, d = h2.shape
    n_experts, _, dff = wg.shape
    n_pairs = 2 * n
    tm = _moe_tile(n_pairs, n_experts)
    fc = _ff_chunk(dff)
    n_tiles = n_pairs // tm + n_experts
    rows = n_tiles * tm

    e_flat = idx.T.reshape(-1)
    order = jnp.argsort(e_flat, stable=True).astype(jnp.int32)
    counts = jnp.sum((e_flat[:, None] == jnp.arange(n_experts, dtype=jnp.int32)[None, :]).astype(jnp.int32), axis=0)
    padded = ((counts + tm - 1) // tm) * tm
    pad_end = jnp.cumsum(padded)
    pad_start = pad_end - padded
    cnt_start = jnp.cumsum(counts) - counts
    tile_first = jnp.arange(n_tiles, dtype=jnp.int32) * tm
    tile_valid = (tile_first < pad_end[-1]).astype(jnp.int32)
    tile_expert = jnp.minimum(jnp.searchsorted(pad_end, tile_first, side="right"), n_experts - 1).astype(jnp.int32)
    last_valid_expert = tile_expert[jnp.maximum(jnp.sum(tile_valid) - 1, 0)]
    tile_expert = jnp.where(tile_valid > 0, tile_expert, last_valid_expert)
    r = jnp.arange(rows, dtype=jnp.int32)
    e_r = jnp.repeat(tile_expert, tm)
    s = r - pad_start[e_r]
    row_valid = (s < counts[e_r]) & (jnp.repeat(tile_valid, tm) > 0)
    pair = order[jnp.clip(cnt_start[e_r] + s, 0, n_pairs - 1)]
    src = jnp.where(row_valid, pair % n, 0).astype(jnp.int32).reshape(n_tiles, tm)
    dst = jnp.where(row_valid, pair, n_pairs + r % tm).astype(jnp.int32).reshape(n_tiles, tm)

    grid_spec = pltpu.PrefetchScalarGridSpec(
        num_scalar_prefetch=2,
        grid=(n_tiles, dff // fc),
        in_specs=[
            pl.BlockSpec(memory_space=pl.ANY),
            pl.BlockSpec(memory_space=pl.ANY),
            pl.BlockSpec(memory_space=pl.ANY),
            pl.BlockSpec((None, d, fc), lambda i, j, te, tv: (te[i], 0, j)),
            pl.BlockSpec((None, d, fc), lambda i, j, te, tv: (te[i], 0, j)),
            pl.BlockSpec((None, fc, d), lambda i, j, te, tv: (te[i], j, 0)),
        ],
        out_specs=pl.BlockSpec(memory_space=pl.ANY),
        scratch_shapes=[
            pltpu.SMEM((2, tm), jnp.int32),
            pltpu.SMEM((2, tm), jnp.int32),
            pltpu.VMEM((2, tm // SUBLANES, SUBLANES, d), F32),
            pltpu.VMEM((tm, d), BF16),
            pltpu.VMEM((2, tm // SUBLANES, SUBLANES, d), F32),
            pltpu.SemaphoreType.DMA((3, 2)),
        ],
    )
    ys = pl.pallas_call(
        functools.partial(_moe_expert_kernel, tm=tm),
        grid_spec=grid_spec,
        out_shape=jax.ShapeDtypeStruct(((n_pairs + tm) // SUBLANES, SUBLANES, d), F32),
        compiler_params=_cparams(("arbitrary", "arbitrary")),
        name="moe_experts",
    )(tile_expert, tile_valid, src, dst, h2.reshape(n // SUBLANES, SUBLANES, d), wg, wu, wd)
    return ys.reshape(n_pairs + tm, d)


def _moe_combine_kernel(*refs, final):
    if final:
        y0_ref, y1_ref, wts_ref, x1_ref, gate_ref, gf_ref, o_ref = refs
    else:
        y0_ref, y1_ref, wts_ref, x1_ref, gate_ref, o_ref = refs
    w = wts_ref[...]
    y = w[:, 0:1] * y0_ref[...] + w[:, 1:2] * y1_ref[...]
    x = x1_ref[...] + gate_ref[...] * y
    o_ref[...] = _rms(x, gf_ref[...]) if final else x


def _moe_combine(ys, wts, x1, gate, g_final):
    b, t_len, d = x1.shape
    n = b * t_len
    tm = min(512, t_len)
    nt = t_len // tm
    final = g_final is not None
    row = lambda bi, ti: (bi, ti, 0)
    in_specs = [
        pl.BlockSpec((tm, d), lambda bi, ti: (bi * nt + ti, 0)),
        pl.BlockSpec((tm, d), lambda bi, ti: (n // tm + bi * nt + ti, 0)),
        pl.BlockSpec((None, tm, 2), row),
        pl.BlockSpec((None, tm, d), row),
        pl.BlockSpec((None, None, 1, d), lambda bi, ti: (bi, 5, 0, 0)),
    ]
    args = [ys, ys, wts, x1, gate]
    if final:
        in_specs.append(pl.BlockSpec((1, d), lambda bi, ti: (0, 0)))
        args.append(g_final)
    return pl.pallas_call(
        functools.partial(_moe_combine_kernel, final=final),
        grid=(b, nt),
        in_specs=in_specs,
        out_specs=pl.BlockSpec((None, tm, d), row),
        out_shape=jax.ShapeDtypeStruct((b, t_len, d), F32),
        compiler_params=_cparams(("arbitrary", "arbitrary")),
        name="moe_combine",
    )(*args)


ROUTE_TOKENS = 256
SEG_ROWS = 128


def _route_tables(idx2, tm, n_experts):
    n = idx2.shape[0]
    ts = min(ROUTE_TOKENS, n)
    ntt = n // ts
    ar = jnp.arange(n_experts, dtype=jnp.int32)[None, :]
    member = ((idx2[:, 0:1] == ar) | (idx2[:, 1:2] == ar)).astype(jnp.int32)
    cnt = member.reshape(ntt, ts, n_experts).sum(axis=1)
    cnt8 = (cnt + SUBLANES - 1) // SUBLANES * SUBLANES
    total = cnt8.sum(axis=0)
    padded = (total + SEG_ROWS + tm - 1) // tm * tm
    gend = jnp.cumsum(padded)
    gstart = gend - padded
    off = gstart[None, :] + jnp.cumsum(cnt8, axis=0) - cnt8
    n_tiles = (2 * n + (SUBLANES - 1) * ntt * n_experts + n_experts * (SEG_ROWS + tm)) // tm + 1
    tile_first = jnp.arange(n_tiles, dtype=jnp.int32) * tm
    tile_valid = (tile_first < gend[-1]).astype(jnp.int32)
    tile_expert = jnp.minimum(jnp.searchsorted(gend, tile_first, side="right"), n_experts - 1).astype(jnp.int32)
    last_valid_expert = tile_expert[jnp.maximum(jnp.sum(tile_valid) - 1, 0)]
    tile_expert = jnp.where(tile_valid > 0, tile_expert, last_valid_expert)
    return (off.astype(jnp.int32).reshape(-1), cnt.astype(jnp.int32).reshape(-1), tile_expert, tile_valid,
            ts, ntt, n_tiles)


def _selection(idx_ref, n_experts, base):
    idx = idx_ref[...]
    ts = idx.shape[0]
    lane = lax.broadcasted_iota(jnp.int32, (ts, LANES), 1)
    member = jnp.logical_or(lane == idx[:, 0:1], lane == idx[:, 1:2])
    before = (lax.broadcasted_iota(jnp.int32, (ts, ts), 1) < lax.broadcasted_iota(jnp.int32, (ts, ts), 0))
    rank = jnp.dot(before.astype(BF16), member.astype(BF16), preferred_element_type=F32)
    col = lax.broadcasted_iota(jnp.int32, (ts, n_experts * SEG_ROWS), 1)
    col_e = lax.shift_right_logical(col, SEG_ROWS.bit_length() - 1)
    col_r = (col & (SEG_ROWS - 1)) + base
    sels = []
    for k in range(2):
        rk = jnp.sum(jnp.where(lane == idx[:, k:k + 1], rank, 0.0), axis=-1, keepdims=True).astype(jnp.int32)
        sels.append(jnp.logical_and(col_e == idx[:, k:k + 1], col_r == rk))
    return sels


def _moe_dispatch_kernel(off_ref, cnt_ref, h_ref, idx_ref, xs_in, xs_hbm, buf_ref, sems, *, n_experts):
    del xs_in
    i = pl.program_id(0)
    slot = lax.rem(i, 2)

    def seg_copy(e, s, base):
        row = pl.multiple_of(off_ref[i * n_experts + e] + base, SUBLANES)
        return pltpu.make_async_copy(buf_ref.at[s, pl.ds(e * SEG_ROWS, SEG_ROWS)],
                                     xs_hbm.at[pl.ds(row, SEG_ROWS)], sems.at[s])

    def compact(base):
        s0, s1 = _selection(idx_ref, n_experts, base)
        sel = jnp.logical_or(s0, s1).astype(BF16)
        return lax.dot_general(sel, h_ref[...], (((0,), (0,)), ((), ())), preferred_element_type=F32)

    def wait_tile(s):
        pltpu.make_async_copy(buf_ref.at[s], xs_hbm.at[pl.ds(0, n_experts * SEG_ROWS)], sems.at[s]).wait()

    for par in (0, 1):
        @pl.when(slot == par)
        def _(par=par):
            buf_ref[par] = compact(0)
            @pl.when(i > 0)
            def _():
                wait_tile(1 - par)
            for e in range(n_experts):
                seg_copy(e, par, 0).start()

    over = cnt_ref[i * n_experts] > SEG_ROWS
    for e in range(1, n_experts):
        over = jnp.logical_or(over, cnt_ref[i * n_experts + e] > SEG_ROWS)

    @pl.when(over)
    def _():
        buf_ref[2] = compact(SEG_ROWS)
        for e in range(n_experts):
            @pl.when(cnt_ref[i * n_experts + e] > SEG_ROWS)
            def _(e=e):
                cp = seg_copy(e, 2, SEG_ROWS)
                cp.start()
                cp.wait()

    @pl.when(i == pl.num_programs(0) - 1)
    def _():
        for par in (0, 1):
            @pl.when(slot == par)
            def _(par=par):
                wait_tile(par)


def _moe_dispatch(h2b, idx2, off, cnt, ts, ntt, rows, n_experts):
    n, d = h2b.shape
    grid_spec = pltpu.PrefetchScalarGridSpec(
        num_scalar_prefetch=2,
        grid=(ntt,),
        in_specs=[
            pl.BlockSpec((ts, d), lambda i, o, c: (i, 0)),
            pl.BlockSpec((ts, 2), lambda i, o, c: (i, 0)),
            pl.BlockSpec(memory_space=pl.ANY),
        ],
        out_specs=pl.BlockSpec(memory_space=pl.ANY),
        scratch_shapes=[pltpu.VMEM((3, n_experts * SEG_ROWS, d), F32), pltpu.SemaphoreType.DMA((3,))],
    )
    return pl.pallas_call(
        functools.partial(_moe_dispatch_kernel, n_experts=n_experts),
        grid_spec=grid_spec,
        out_shape=jax.ShapeDtypeStruct((rows, d), F32),
        input_output_aliases={4: 0},
        compiler_params=_cparams(("arbitrary",)),
        name="moe_dispatch",
    )(off, cnt, h2b, idx2, jnp.zeros((rows, d), F32))


def _moe_grouped_kernel(te_ref, tv_ref, x_ref, wg_ref, wu_ref, wd_ref, y_ref, xb_ref, acc_ref):
    i = pl.program_id(0)
    j = pl.program_id(1)
    nj = pl.num_programs(1)

    @pl.when(tv_ref[i] > 0)
    def _():
        @pl.when(j == 0)
        def _():
            xb_ref[...] = x_ref[...].astype(BF16)
            acc_ref[...] = jnp.zeros_like(acc_ref)

        xb = xb_ref[...]
        gp = jnp.dot(xb, wg_ref[...].astype(BF16), preferred_element_type=F32)
        up = jnp.dot(xb, wu_ref[...].astype(BF16), preferred_element_type=F32)
        act = (_silu(gp) * up).astype(BF16)
        acc_ref[...] += jnp.dot(act, wd_ref[...].astype(BF16), preferred_element_type=F32)

        @pl.when(j == nj - 1)
        def _():
            y_ref[...] = acc_ref[...]

    @pl.when(jnp.logical_and(tv_ref[i] <= 0, j == nj - 1))
    def _():
        y_ref[...] = jnp.zeros_like(y_ref)


def _moe_grouped(xs, tile_expert, tile_valid, wg, wu, wd, tm):
    rows, d = xs.shape
    n_experts, _, dff = wg.shape
    fc = _ff_chunk(dff)
    grid_spec = pltpu.PrefetchScalarGridSpec(
        num_scalar_prefetch=2,
        grid=(rows // tm, dff // fc),
        in_specs=[
            pl.BlockSpec((tm, d), lambda i, j, te, tv: (i, 0)),
            pl.BlockSpec((None, d, fc), lambda i, j, te, tv: (te[i], 0, j)),
            pl.BlockSpec((None, d, fc), lambda i, j, te, tv: (te[i], 0, j)),
            pl.BlockSpec((None, fc, d), lambda i, j, te, tv: (te[i], j, 0)),
        ],
        out_specs=pl.BlockSpec((tm, d), lambda i, j, te, tv: (i, 0)),
        scratch_shapes=[pltpu.VMEM((tm, d), BF16), pltpu.VMEM((tm, d), F32)],
    )
    return pl.pallas_call(
        _moe_grouped_kernel,
        grid_spec=grid_spec,
        out_shape=jax.ShapeDtypeStruct((rows, d), F32),
        compiler_params=_cparams(("arbitrary", "arbitrary")),
        name="moe_grouped",
    )(tile_expert, tile_valid, xs, wg, wu, wd)


def _moe_undispatch_kernel(*refs, n_experts, final):
    if final:
        off_ref, cnt_ref, idx_ref, wts_ref, x1_ref, gate_ref, gf_ref, y_hbm, o_ref, ybuf_ref, sems = refs
    else:
        off_ref, cnt_ref, idx_ref, wts_ref, x1_ref, gate_ref, y_hbm, o_ref, ybuf_ref, sems = refs
    i = pl.program_id(0)
    nt = pl.num_programs(0)
    slot = lax.rem(i, 2)
    ts = idx_ref.shape[0]

    def seg_copy(tile, e, s, base):
        row = pl.multiple_of(off_ref[tile * n_experts + e] + base, SUBLANES)
        return pltpu.make_async_copy(y_hbm.at[pl.ds(row, SEG_ROWS)],
                                     ybuf_ref.at[s, pl.ds(e * SEG_ROWS, SEG_ROWS)], sems.at[s])

    def fetch(tile, s):
        for e in range(n_experts):
            seg_copy(tile, e, s, 0).start()

    def wait_tile(s):
        pltpu.make_async_copy(y_hbm.at[pl.ds(0, n_experts * SEG_ROWS)], ybuf_ref.at[s], sems.at[s]).wait()

    def gather(s, base):
        s0, s1 = _selection(idx_ref, n_experts, base)
        sel = jnp.concatenate([s0.astype(BF16), s1.astype(BF16)], axis=0)
        z = jnp.dot(sel, ybuf_ref[s].astype(BF16), preferred_element_type=F32)
        w = wts_ref[...]
        return w[:, 0:1] * z[:ts] + w[:, 1:2] * z[ts:]

    over = cnt_ref[i * n_experts] > SEG_ROWS
    for e in range(1, n_experts):
        over = jnp.logical_or(over, cnt_ref[i * n_experts + e] > SEG_ROWS)

    for par in (0, 1):
        @pl.when(slot == par)
        def _(par=par):
            @pl.when(i == 0)
            def _():
                fetch(0, 0)

            @pl.when(i + 1 < nt)
            def _():
                fetch(i + 1, 1 - par)

            wait_tile(par)
            x = x1_ref[...] + gate_ref[...] * gather(par, 0)
            o_ref[...] = x

            @pl.when(over)
            def _():
                ybuf_ref[par] = jnp.zeros((n_experts * SEG_ROWS, ybuf_ref.shape[-1]), F32)
                for e in range(n_experts):
                    @pl.when(cnt_ref[i * n_experts + e] > SEG_ROWS)
                    def _(e=e):
                        cp = seg_copy(i, e, par, SEG_ROWS)
                        cp.start()
                        cp.wait()
                o_ref[...] = o_ref[...] + gate_ref[...] * gather(par, SEG_ROWS)

            if final:
                o_ref[...] = _rms(o_ref[...], gf_ref[...])


def _moe_undispatch(yexp, idx2, wts2, x1, gate, g_final, off, cnt, ts, n_experts):
    b, t_len, d = x1.shape
    per_b = t_len // ts
    final = g_final is not None
    in_specs = [
        pl.BlockSpec((ts, 2), lambda i, o, c: (i, 0)),
        pl.BlockSpec((ts, 2), lambda i, o, c: (i, 0)),
        pl.BlockSpec((None, ts, d), lambda i, o, c: (i // per_b, i % per_b, 0)),
        pl.BlockSpec((None, None, 1, d), lambda i, o, c: (i // per_b, 5, 0, 0)),
    ]
    args = [idx2, wts2, x1, gate]
    if final:
        in_specs.append(pl.BlockSpec((1, d), lambda i, o, c: (0, 0)))
        args.append(g_final)
    in_specs.append(pl.BlockSpec(memory_space=pl.ANY))
    args.append(yexp)
    grid_spec = pltpu.PrefetchScalarGridSpec(
        num_scalar_prefetch=2,
        grid=(b * per_b,),
        in_specs=in_specs,
        out_specs=pl.BlockSpec((None, ts, d), lambda i, o, c: (i // per_b, i % per_b, 0)),
        scratch_shapes=[pltpu.VMEM((2, n_experts * SEG_ROWS, d), F32), pltpu.SemaphoreType.DMA((2,))],
    )
    return pl.pallas_call(
        functools.partial(_moe_undispatch_kernel, n_experts=n_experts, final=final),
        grid_spec=grid_spec,
        out_shape=jax.ShapeDtypeStruct((b, t_len, d), F32),
        compiler_params=_cparams(("arbitrary",)),
        name="moe_undispatch",
    )(off, cnt, *args)


def _moe_routed(h2b, idx2, wts2, x1, gate, g_final, wg, wu, wd):
    n, d = h2b.shape
    n_experts = wg.shape[0]
    tm = _moe_tile(2 * n, n_experts)
    off, cnt, tile_expert, tile_valid, ts, ntt, n_tiles = _route_tables(idx2, tm, n_experts)
    xs = _moe_dispatch(h2b, idx2, off, cnt, ts, ntt, n_tiles * tm, n_experts)
    yexp = _moe_grouped(xs, tile_expert, tile_valid, wg, wu, wd, tm)
    return _moe_undispatch(yexp, idx2, wts2, x1, gate, g_final, off, cnt, ts, n_experts)


def _moe_small_kernel(*refs, final):
    if final:
        h_ref, idx_ref, wts_ref, x1_ref, gate_ref, wg_ref, wu_ref, wd_ref, gf_ref, o_ref, acc_ref = refs
    else:
        h_ref, idx_ref, wts_ref, x1_ref, gate_ref, wg_ref, wu_ref, wd_ref, o_ref, acc_ref = refs
    e = pl.program_id(0)
    j = pl.program_id(1)

    @pl.when((e == 0) & (j == 0))
    def _():
        acc_ref[...] = jnp.zeros_like(acc_ref)

    hb = h_ref[...].astype(BF16)
    gp = jnp.dot(hb, wg_ref[...].astype(BF16), preferred_element_type=F32)
    up = jnp.dot(hb, wu_ref[...].astype(BF16), preferred_element_type=F32)
    act = (_silu(gp) * up).astype(BF16)
    y = jnp.dot(act, wd_ref[...].astype(BF16), preferred_element_type=F32)
    idx = idx_ref[...]
    wts = wts_ref[...]
    ge = (jnp.where(idx[:, 0:1] == e, wts[:, 0:1], 0.0) + jnp.where(idx[:, 1:2] == e, wts[:, 1:2], 0.0))
    acc_ref[...] += ge * y

    @pl.when((e == pl.num_programs(0) - 1) & (j == pl.num_programs(1) - 1))
    def _():
        x = x1_ref[...] + gate_ref[...] * acc_ref[...]
        o_ref[...] = _rms(x, gf_ref[...]) if final else x


def _moe_small(h2, idx, wts, x1, gate_rows, wg, wu, wd, g_final):
    rws, d = x1.shape
    n_experts, _, dff = wg.shape
    fc = _ff_chunk(dff)
    final = g_final is not None
    full = lambda e, j: (0, 0)
    in_specs = [
        pl.BlockSpec((rws, d), full), pl.BlockSpec((rws, 2), full), pl.BlockSpec((rws, 2), full),
        pl.BlockSpec((rws, d), full),
        pl.BlockSpec((None, rws, d), lambda e, j: (5, 0, 0)),
        pl.BlockSpec((None, d, fc), lambda e, j: (e, 0, j)),
        pl.BlockSpec((None, d, fc), lambda e, j: (e, 0, j)),
        pl.BlockSpec((None, fc, d), lambda e, j: (e, j, 0)),
    ]
    args = [h2, idx, wts, x1, gate_rows, wg, wu, wd]
    if final:
        in_specs.append(pl.BlockSpec((1, d), full))
        args.append(g_final)
    return pl.pallas_call(
        functools.partial(_moe_small_kernel, final=final),
        grid=(n_experts, dff // fc),
        in_specs=in_specs,
        out_specs=pl.BlockSpec((rws, d), full),
        out_shape=jax.ShapeDtypeStruct((rws, d), F32),
        scratch_shapes=[pltpu.VMEM((rws, d), F32)],
        compiler_params=_cparams(("arbitrary", "arbitrary")),
        name="moe_small",
    )(*args)


def _rope_tables(pos, head_dim):
    half = head_dim // 2
    inv_freq = 1.0 / (ROPE_THETA ** (jnp.arange(half, dtype=F32) / half))
    ang = pos.astype(F32)[:, None] * inv_freq[None, :]
    cos = jnp.tile(jnp.cos(ang), (1, LANES // half))
    sin = jnp.tile(jnp.sin(ang), (1, LANES // half))
    lane = jnp.arange(LANES)
    sign = jnp.where((lane % head_dim) < half, -1.0, 1.0).astype(F32)
    return cos, sin * sign[None, :]


def kernel(x_prompt, x_sample, c_prompt, c_sample, cache_k, cache_v, state_conv, w_ada, b_ada, g_mix, g_ffn, w_in, w_dw, b_dw, g_conv_ln, b_conv_ln, lambda_q1, lambda_k1, lambda_q2, lambda_k2, g_subln, w_out, w_ffn_gate, w_ffn_up, w_ffn_down, w_router, w_moe_gate, w_moe_up, w_moe_down, g_final):
    depth, d, _ = w_in.shape
    bp, tp, _ = x_prompt.shape
    bs, ts, _ = x_sample.shape
    past = cache_k.shape[2]
    n_sub, head_dim = cache_k.shape[3], cache_k.shape[4]
    att_dim = n_sub * head_dim
    width, conv_dim = w_dw.shape[1], w_dw.shape[2]
    n_experts = w_router.shape[2]
    assert tp % Q_BLOCK == 0 and d % LANES == 0 and conv_dim % LANES == 0 and att_dim % LANES == 0
    assert 2 * head_dim == LANES and width - 1 <= CONV_HALO and n_experts <= LANES

    mod_all = _adaln(jnp.concatenate([c_prompt, c_sample], axis=0), w_ada, b_ada)
    cos_p, sin_p = _rope_tables(jnp.arange(tp), head_dim)
    cos_s, sin_s = _rope_tables(past + jnp.arange(ts), head_dim)
    hist_p = jnp.zeros((bp, CONV_HALO, conv_dim), F32)
    pad_rows = CONV_HALO - (width - 1)
    cache_k2 = cache_k.reshape(depth, bs, past, att_dim)
    cache_v2 = cache_v.reshape(depth, bs, past, att_dim)
    g_fin = g_final.reshape(1, d)

    xp, xs = x_prompt, x_sample
    outs = {k: [] for k in ("kp", "vp", "cp", "ks", "vs", "cs")}
    for l in range(depth):
        last = l == depth - 1
        moe = l % 2 == 1
        lam_init = 0.8 - 0.6 * math.exp(-0.3 * l)
        w_in_b = w_in[l].astype(BF16)
        wo_b = w_out[l].astype(BF16)
        gm = g_mix[l].reshape(1, d)
        gf = g_ffn[l].reshape(1, d)
        lam_args = [a[l].reshape(1, head_dim) for a in (lambda_q1, lambda_k1, lambda_q2, lambda_k2)]
        gsub = g_subln[l].reshape(1, 2 * head_dim)
        conv_args = (w_dw[l], b_dw[l].reshape(1, conv_dim), g_conv_ln[l].reshape(1, conv_dim),
                     b_conv_ln[l].reshape(1, conv_dim))
        mod_p = mod_all[l, :bp].reshape(bp, 6, 1, d)
        mod_s = mod_all[l, bp:].reshape(bs, 6, 1, d)
        mod_s_rows = jnp.broadcast_to(mod_all[l, bp:].reshape(bs, 1, 6, d), (bs, ts, 6, d))
        mod_s_rows = mod_s_rows.transpose(2, 0, 1, 3).reshape(6, 1, bs * ts, d)
        if moe:
            i = l // 2
            wr_b = jnp.pad(w_router[i], ((0, 0), (0, LANES - n_experts))).astype(BF16)
        else:
            i = l // 2
            wr_b = None
            wg_b, wu_b, wd_b = (w_ffn_gate[i].astype(BF16), w_ffn_up[i].astype(BF16), w_ffn_down[i].astype(BF16))

        kp, vp, cp, qb, kb, vt, cv = _mixer_in(xp, mod_p, gm, w_in_b, cos_p, sin_p, hist_p, *conv_args,
                                               head_dim=head_dim, transpose_v=True)
        att = _attn_prompt(qb, kb, vt, *lam_args, gsub, head_dim=head_dim, lam_init=lam_init)
        res = _out_proj(cv, att, xp, mod_p, mod_p, mod_p, gf, wo_b, wr_b, per_row_mod=False, n_experts=n_experts)
        if moe:
            x1, h2, idx, wts = res
            xp = _moe_routed(h2.reshape(bp * tp, d), idx.reshape(bp * tp, 2), wts.reshape(bp * tp, 2), x1, mod_p,
                             g_fin if last else None, w_moe_gate[i], w_moe_up[i], w_moe_down[i])
        else:
            x1, h2 = res
            xp = _dense_ffn(h2, x1, mod_p, wg_b, wu_b, wd_b, g_fin if last else None, per_row_mod=False)
        outs["kp"].append(kp.reshape(bp, tp, n_sub, head_dim))
        outs["vp"].append(vp.reshape(bp, tp, n_sub // 2, 2 * head_dim))
        outs["cp"].append(cp)

        hist_s = jnp.pad(state_conv[l], ((0, 0), (pad_rows, 0), (0, 0)))
        ks, vs, cs, qb, kb, vb, cv = _mixer_in(xs, mod_s, gm, w_in_b, cos_s, sin_s, hist_s, *conv_args,
                                               head_dim=head_dim, transpose_v=False)
        att = _attn_sample(qb, kb, vb, cache_k2[l], cache_v2[l], *lam_args, gsub,
                           head_dim=head_dim, lam_init=lam_init)
        rows = bs * ts
        res = _out_proj(cv.reshape(1, rows, conv_dim), att.reshape(1, rows, att_dim), xs.reshape(1, rows, d),
                        mod_s_rows, mod_s_rows, mod_s_rows, gf, wo_b, wr_b, per_row_mod=True, n_experts=n_experts)
        if moe:
            x1, h2, idx, wts = res
            xs = _moe_small(h2.reshape(rows, d), idx.reshape(rows, 2), wts.reshape(rows, 2), x1.reshape(rows, d),
                            mod_s_rows.reshape(6, rows, d), w_moe_gate[i], w_moe_up[i], w_moe_down[i],
                            g_fin if last else None)
        else:
            x1, h2 = res
            xs = _dense_ffn(h2, x1, mod_s_rows, wg_b, wu_b, wd_b, g_fin if last else None, per_row_mod=True)
        xs = xs.reshape(bs, ts, d)
        outs["ks"].append(ks.reshape(bs, ts, n_sub, head_dim))
        outs["vs"].append(vs.reshape(bs, ts, n_sub // 2, 2 * head_dim))
        outs["cs"].append(cs)

    return (xp, xs, jnp.stack(outs["kp"]), jnp.stack(outs["vp"]), jnp.stack(outs["cp"]),
            jnp.stack(outs["ks"]), jnp.stack(outs["vs"]), jnp.stack(outs["cs"]))
```

```python
import functools
import math

import jax
import jax.numpy as jnp
from jax import lax
from jax.experimental import pallas as pl
from jax.experimental.pallas import tpu as pltpu

F32 = jnp.float32
BF16 = jnp.bfloat16

EPS = 1e-6
CHUNK = 64
ROPE_THETA = 10000.0
NEG_INF = -1e30

LANES = 128
SUBLANES = 8
VMEM_LIMIT_BYTES = 56 * 1024 * 1024

CONV_HALO = 32
Q_BLOCK = 256
ADA_COLS = 1536
MIXER_PART_ROWS = 128
REDUCE_ROWS = 64
LOG2E = 1.4426950408889634


def _cparams(sem):
    return pltpu.CompilerParams(dimension_semantics=sem, vmem_limit_bytes=VMEM_LIMIT_BYTES)


def _sigmoid(x):
    return 1.0 / (1.0 + jnp.exp(-x))


def _silu(x):
    return x * _sigmoid(x)


def _rms(x, g):
    return x * lax.rsqrt(jnp.mean(x * x, axis=-1, keepdims=True) + EPS) * g


def _col_reduce(x, op):
    rows, cols = x.shape
    if rows > REDUCE_ROWS and rows % REDUCE_ROWS == 0:
        x = op(x.reshape(rows // REDUCE_ROWS, REDUCE_ROWS, cols), axis=0)
    return op(x, axis=0, keepdims=True)


def _adaln_kernel(c_ref, w_ref, b_ref, o_ref):
    s = _silu(c_ref[...]).astype(BF16)
    o_ref[...] = jnp.dot(s, w_ref[...].astype(BF16), preferred_element_type=F32) + b_ref[...]


def _adaln(c_all, w_ada, b_ada):
    depth, d, n = w_ada.shape
    rows = c_all.shape[0]
    nb = ADA_COLS if n % ADA_COLS == 0 else n
    return pl.pallas_call(
        _adaln_kernel,
        grid=(depth, n // nb),
        in_specs=[
            pl.BlockSpec((rows, d), lambda l, j: (0, 0)),
            pl.BlockSpec((None, d, nb), lambda l, j: (l, 0, j)),
            pl.BlockSpec((None, 1, nb), lambda l, j: (l, 0, j)),
        ],
        out_specs=pl.BlockSpec((None, rows, nb), lambda l, j: (l, 0, j)),
        out_shape=jax.ShapeDtypeStruct((depth, rows, n), F32),
        compiler_params=_cparams(("arbitrary", "arbitrary")),
        name="adaln",
    )(c_all, w_ada, b_ada.reshape(depth, 1, n))


def _mixer_in_kernel(x_ref, shift_ref, scale_ref, g_ref, w_ref, cos_ref, sin_ref, hist_ref,
                     wdw_ref, bdw_ref, gln_ref, bln_ref, k_all_ref, v_all_ref,
                     k_ref, v_ref, nc_ref, qb_ref, kb_ref, vx_ref, cv_ref,
                     ext_ref, craw_ref, z_ref, *, tm, conv_dim, att_dim, head_dim, width, att_scale,
                     transpose_v, row_chunk, part_rows):
    del k_all_ref, v_all_ref
    t = pl.program_id(1)
    nt = pl.num_programs(1)

    @pl.when(t == 0)
    def _():
        for s in range(SUBLANES):
            ext_ref[s, 0:CONV_HALO - s, :] = hist_ref[s:CONV_HALO, :]

    first_tap = CONV_HALO - (width - 1)
    lane = lax.broadcasted_iota(jnp.int32, (1, LANES), 1)
    first_half = (lane & (head_dim - 1)) < (head_dim // 2)
    n_parts = tm // part_rows

    def project(p):
        rows = slice(p * part_rows, (p + 1) * part_rows)
        h = _rms(x_ref[rows, :], g_ref[...]) * (1.0 + scale_ref[...]) + shift_ref[...]
        z_ref[rows, :] = jnp.dot(h.astype(BF16), w_ref[...], preferred_element_type=F32)

    def finish(p):
        lo = p * part_rows
        rows = slice(lo, lo + part_rows)
        u = z_ref[rows, 0:conv_dim] * _sigmoid(z_ref[rows, conv_dim:2 * conv_dim])
        for s in range(SUBLANES):
            ext_ref[s, CONV_HALO - s + lo:CONV_HALO - s + lo + part_rows, :] = u
        for r0 in range(lo, lo + part_rows, row_chunk):
            for lb in range(conv_dim // LANES):
                sl = slice(lb * LANES, (lb + 1) * LANES)
                acc = jnp.broadcast_to(bdw_ref[:, sl], (row_chunk, LANES))
                for w in range(width):
                    s = (first_tap + w) % SUBLANES
                    a0 = r0 + first_tap + w - s
                    acc = acc + ext_ref[s, a0:a0 + row_chunk, sl] * wdw_ref[w:w + 1, sl]
                craw_ref[r0:r0 + row_chunk, sl] = acc
        cr = craw_ref[rows, :]
        mu = jnp.mean(cr, axis=-1, keepdims=True)
        dlt = cr - mu
        var = jnp.mean(dlt * dlt, axis=-1, keepdims=True)
        y = dlt * lax.rsqrt(var + EPS) * gln_ref[...] + bln_ref[...]
        cv_ref[rows, :] = _silu(y).astype(BF16)

        cos = cos_ref[rows, :]
        sin = sin_ref[rows, :]

        def rope(xb):
            partner = jnp.where(first_half,
                                pltpu.roll(xb, LANES - head_dim // 2, 1),
                                pltpu.roll(xb, head_dim // 2, 1))
            return xb * cos + partner * sin

        o = 2 * conv_dim
        for lb in range(att_dim // LANES):
            sl = slice(lb * LANES, (lb + 1) * LANES)
            qblk = rope(z_ref[rows, o + lb * LANES:o + (lb + 1) * LANES])
            qb_ref[rows, sl] = (qblk * att_scale).astype(BF16)
            kblk = rope(z_ref[rows, o + att_dim + lb * LANES:o + att_dim + (lb + 1) * LANES])
            k_ref[rows, sl] = kblk
            kb_ref[rows, sl] = kblk.astype(BF16)
        v = z_ref[rows, o + 2 * att_dim:o + 3 * att_dim]
        v_ref[rows, :] = v
        if transpose_v:
            vx_ref[:, rows] = v.T.astype(BF16)
        else:
            vx_ref[rows, :] = v.astype(BF16)

    project(0)
    for p in range(n_parts):
        if p + 1 < n_parts:
            project(p + 1)
        finish(p)

    @pl.when(t == nt - 1)
    def _():
        nc_ref[...] = ext_ref[0, tm + first_tap:tm + CONV_HALO, :]

    for s in range(SUBLANES):
        tail = ext_ref[s, tm:tm + CONV_HALO - s, :]
        ext_ref[s, 0:CONV_HALO - s, :] = tail


def _mixer_in(x, mod, g, w_in_b, cos_t, sin_t, hist, w_dw, b_dw, g_ln, b_ln, k_all, v_all, *, layer, head_dim,
              transpose_v):
    b, t_len, d = x.shape
    conv_dim = w_dw.shape[1]
    width = w_dw.shape[0]
    att_dim = (w_in_b.shape[1] - 2 * conv_dim) // 3
    tm = min(512, t_len)
    row_chunk = min(32, tm)
    nt = t_len // tm
    kern = functools.partial(
        _mixer_in_kernel, tm=tm, conv_dim=conv_dim, att_dim=att_dim, head_dim=head_dim, width=width,
        att_scale=head_dim ** -0.5 * LOG2E, transpose_v=transpose_v, row_chunk=row_chunk,
        part_rows=min(MIXER_PART_ROWS, tm))
    row = lambda bi, ti: (bi, ti, 0)
    vec = lambda bi, ti: (0, 0)
    if transpose_v:
        vx_spec = pl.BlockSpec((None, att_dim, tm), lambda bi, ti: (bi, 0, ti))
        vx_shape = jax.ShapeDtypeStruct((b, att_dim, t_len), BF16)
    else:
        vx_spec = pl.BlockSpec((None, tm, att_dim), row)
        vx_shape = jax.ShapeDtypeStruct((b, t_len, att_dim), BF16)
    return pl.pallas_call(
        kern,
        grid=(b, nt),
        in_specs=[
            pl.BlockSpec((None, tm, d), row),
            pl.BlockSpec((None, None, 1, d), lambda bi, ti: (bi, 0, 0, 0)),
            pl.BlockSpec((None, None, 1, d), lambda bi, ti: (bi, 1, 0, 0)),
            pl.BlockSpec((1, d), vec),
            pl.BlockSpec(w_in_b.shape, vec),
            pl.BlockSpec((tm, LANES), lambda bi, ti: (ti, 0)),
            pl.BlockSpec((tm, LANES), lambda bi, ti: (ti, 0)),
            pl.BlockSpec((None, CONV_HALO, conv_dim), lambda bi, ti: (bi, 0, 0)),
            pl.BlockSpec((width, conv_dim), vec),
            pl.BlockSpec((1, conv_dim), vec),
            pl.BlockSpec((1, conv_dim), vec),
            pl.BlockSpec((1, conv_dim), vec),
            pl.BlockSpec(memory_space=pl.ANY),
            pl.BlockSpec(memory_space=pl.ANY),
        ],
        out_specs=[
            pl.BlockSpec((None, None, tm, att_dim), lambda bi, ti: (layer, bi, ti, 0)),
            pl.BlockSpec((None, None, tm, att_dim), lambda bi, ti: (layer, bi, ti, 0)),
            pl.BlockSpec((None, width - 1, conv_dim), lambda bi, ti: (bi, 0, 0)),
            pl.BlockSpec((None, tm, att_dim), row),
            pl.BlockSpec((None, tm, att_dim), row),
            vx_spec,
            pl.BlockSpec((None, tm, conv_dim), row),
        ],
        out_shape=[
            jax.ShapeDtypeStruct(k_all.shape, F32),
            jax.ShapeDtypeStruct(v_all.shape, F32),
            jax.ShapeDtypeStruct((b, width - 1, conv_dim), F32),
            jax.ShapeDtypeStruct((b, t_len, att_dim), BF16),
            jax.ShapeDtypeStruct((b, t_len, att_dim), BF16),
            vx_shape,
            jax.ShapeDtypeStruct((b, t_len, conv_dim), BF16),
        ],
        scratch_shapes=[
            pltpu.VMEM((SUBLANES, CONV_HALO + tm, conv_dim), F32),
            pltpu.VMEM((tm, conv_dim), F32),
            pltpu.VMEM((tm, w_in_b.shape[1]), F32),
        ],
        input_output_aliases={12: 0, 13: 1},
        compiler_params=_cparams(("arbitrary", "arbitrary")),
        name="mixer_in",
    )(x, mod, mod, g, w_in_b, cos_t, sin_t, hist, w_dw, b_dw, g_ln, b_ln, k_all, v_all)


def _lambda(lq1_ref, lk1_ref, lq2_ref, lk2_ref, lam_init):
    a = jnp.sum(lq1_ref[...] * lk1_ref[...], axis=-1, keepdims=True)
    b = jnp.sum(lq2_ref[...] * lk2_ref[...], axis=-1, keepdims=True)
    return jnp.exp(a) - jnp.exp(b) + lam_init


def _attn_prompt_kernel(q_ref, k_ref, vt_ref, lq1_ref, lk1_ref, lq2_ref, lk2_ref, gsub_ref, o_ref,
                        s_ref, e_ref, *, t_len, head_dim, lam_init):
    lam = _lambda(lq1_ref, lk1_ref, lq2_ref, lk2_ref, lam_init)
    n_pairs = q_ref.shape[-1] // LANES
    lane = lax.broadcasted_iota(jnp.int32, (1, LANES), 1)
    low = lane < head_dim
    kc = lax.broadcasted_iota(jnp.int32, (Q_BLOCK, Q_BLOCK), 0) // CHUNK
    qc = lax.broadcasted_iota(jnp.int32, (Q_BLOCK, Q_BLOCK), 1) // CHUNK
    visible = kc <= qc
    nt_dims = (((1,), (1,)), ((), ()))

    units = [(j, hp, sub) for j in range(t_len // Q_BLOCK) for hp in range(n_pairs) for sub in range(2)]

    def scores(n):
        j, hp, sub = units[n]
        tk = (j + 1) * Q_BLOCK
        sl = slice(hp * LANES, (hp + 1) * LANES)
        qp = q_ref[j * Q_BLOCK:tk, sl]
        qm = jnp.where(low if sub == 0 else jnp.logical_not(low), qp, jnp.zeros_like(qp))
        s = lax.dot_general(k_ref[0:tk, sl], qm, nt_dims, preferred_element_type=F32)
        diag = jnp.where(visible, s[j * Q_BLOCK:, :], NEG_INF)
        s = diag if j == 0 else jnp.concatenate([s[:j * Q_BLOCK, :], diag], axis=0)
        s_ref[n % 2, 0:tk, :] = s
        return _col_reduce(s, jnp.max)

    def probs(n, m):
        tk = (units[n][0] + 1) * Q_BLOCK
        l = None
        for r0 in range(0, tk, Q_BLOCK):
            e = jnp.exp2(s_ref[n % 2, r0:r0 + Q_BLOCK, :] - m)
            e_ref[n % 2, r0:r0 + Q_BLOCK, :] = e.astype(BF16)
            part = jnp.sum(e.reshape(Q_BLOCK // REDUCE_ROWS, REDUCE_ROWS, Q_BLOCK), axis=0)
            l = part if l is None else l + part
        return jnp.sum(l, axis=0, keepdims=True)

    def values(n):
        j, hp, _ = units[n]
        tk = (j + 1) * Q_BLOCK
        sl = slice(hp * LANES, (hp + 1) * LANES)
        return jnp.dot(vt_ref[sl, 0:tk], e_ref[n % 2, 0:tk, :], preferred_element_type=F32)

    maxes, sums, outs = {}, {}, {}
    for step in range(len(units) + 2):
        if step < len(units):
            maxes[step] = scores(step)
        if 0 <= step - 1 < len(units):
            sums[step - 1] = probs(step - 1, maxes.pop(step - 1))
        n = step - 2
        if 0 <= n < len(units):
            outs[n] = values(n)
            j, hp, sub = units[n]
            if sub == 1:
                ot = outs.pop(n - 1) * (1.0 / sums.pop(n - 1)) - outs.pop(n) * (lam / sums.pop(n))
                ms = jnp.mean(ot * ot, axis=0, keepdims=True)
                on = (ot * lax.rsqrt(ms + EPS)).T * (gsub_ref[...] * (1.0 - lam_init))
                o_ref[j * Q_BLOCK:(j + 1) * Q_BLOCK, hp * LANES:(hp + 1) * LANES] = on.astype(BF16)


def _attn_prompt(qb, kb, vt, lq1, lk1, lq2, lk2, gsub, *, head_dim, lam_init):
    b, t_len, a = qb.shape
    kern = functools.partial(_attn_prompt_kernel, t_len=t_len, head_dim=head_dim, lam_init=lam_init)
    vec = lambda bi: (0, 0)
    return pl.pallas_call(
        kern,
        grid=(b,),
        in_specs=[
            pl.BlockSpec((None, t_len, a), lambda bi: (bi, 0, 0)),
            pl.BlockSpec((None, t_len, a), lambda bi: (bi, 0, 0)),
            pl.BlockSpec((None, a, t_len), lambda bi: (bi, 0, 0)),
            pl.BlockSpec((1, head_dim), vec), pl.BlockSpec((1, head_dim), vec),
            pl.BlockSpec((1, head_dim), vec), pl.BlockSpec((1, head_dim), vec),
            pl.BlockSpec((1, 2 * head_dim), vec),
        ],
        out_specs=pl.BlockSpec((None, t_len, a), lambda bi: (bi, 0, 0)),
        out_shape=jax.ShapeDtypeStruct((b, t_len, a), BF16),
        scratch_shapes=[pltpu.VMEM((2, t_len, Q_BLOCK), F32), pltpu.VMEM((2, t_len, Q_BLOCK), BF16)],
        compiler_params=_cparams(("arbitrary",)),
        name="attn_prompt",
    )(qb, kb, vt, lq1, lk1, lq2, lk2, gsub)


def _attn_sample_kernel(q_ref, kn_ref, vn_ref, kc_ref, vc_ref, lq1_ref, lk1_ref, lq2_ref, lk2_ref, gsub_ref,
                        o_ref, *, head_dim, lam_init):
    lam = _lambda(lq1_ref, lk1_ref, lq2_ref, lk2_ref, lam_init)
    n_pairs = q_ref.shape[-1] // LANES
    lane = lax.broadcasted_iota(jnp.int32, (1, LANES), 1)
    low = lane < head_dim
    nt_dims = (((1,), (1,)), ((), ()))
    for hp in range(n_pairs):
        sl = slice(hp * LANES, (hp + 1) * LANES)
        qp = q_ref[:, sl]
        kc = kc_ref[:, sl].astype(BF16)
        vc = vc_ref[:, sl].astype(BF16)
        kn = kn_ref[:, sl]
        vn = vn_ref[:, sl]
        parts = []
        for sub in range(2):
            qm = jnp.where(low if sub == 0 else jnp.logical_not(low), qp, jnp.zeros_like(qp))
            sc = lax.dot_general(qm, kc, nt_dims, preferred_element_type=F32)
            sn = lax.dot_general(qm, kn, nt_dims, preferred_element_type=F32)
            m = jnp.maximum(jnp.max(sc, axis=-1, keepdims=True), jnp.max(sn, axis=-1, keepdims=True))
            ec = jnp.exp2(sc - m)
            en = jnp.exp2(sn - m)
            l = jnp.sum(ec, axis=-1, keepdims=True) + jnp.sum(en, axis=-1, keepdims=True)
            o = (jnp.dot(ec.astype(BF16), vc, preferred_element_type=F32)
                 + jnp.dot(en.astype(BF16), vn, preferred_element_type=F32))
            parts.append((o, l))
        oh = parts[0][0] * (1.0 / parts[0][1]) - parts[1][0] * (lam / parts[1][1])
        on = _rms(oh, gsub_ref[...]) * (1.0 - lam_init)
        o_ref[:, sl] = on.astype(BF16)


def _attn_sample(qb, kb, vb, cache_k, cache_v, lq1, lk1, lq2, lk2, gsub, *, head_dim, lam_init):
    b, tq, a = qb.shape
    past = cache_k.shape[1]
    kern = functools.partial(_attn_sample_kernel, head_dim=head_dim, lam_init=lam_init)
    vec = lambda bi: (0, 0)
    row = lambda bi: (bi, 0, 0)
    return pl.pallas_call(
        kern,
        grid=(b,),
        in_specs=[
            pl.BlockSpec((None, tq, a), row), pl.BlockSpec((None, tq, a), row), pl.BlockSpec((None, tq, a), row),
            pl.BlockSpec((None, past, a), row), pl.BlockSpec((None, past, a), row),
            pl.BlockSpec((1, head_dim), vec), pl.BlockSpec((1, head_dim), vec),
            pl.BlockSpec((1, head_dim), vec), pl.BlockSpec((1, head_dim), vec),
            pl.BlockSpec((1, 2 * head_dim), vec),
        ],
        out_specs=pl.BlockSpec((None, tq, a), row),
        out_shape=jax.ShapeDtypeStruct((b, tq, a), BF16),
        compiler_params=_cparams(("arbitrary",)),
        name="attn_sample",
    )(qb, kb, vb, cache_k, cache_v, lq1, lk1, lq2, lk2, gsub)


def _out_proj_kernel(*refs, moe, n_experts):
    if moe:
        (cv_ref, att_ref, x_ref, gate_ref, shift_ref, scale_ref, g_ref, wo_ref, wr_ref,
         x1_ref, h2_ref, idx_ref, wts_ref) = refs
    else:
        cv_ref, att_ref, x_ref, gate_ref, shift_ref, scale_ref, g_ref, wo_ref, x1_ref, h2_ref = refs
    c = cv_ref.shape[-1]
    mix = (jnp.dot(cv_ref[...], wo_ref[0:c, :], preferred_element_type=F32)
           + jnp.dot(att_ref[...], wo_ref[c:, :], preferred_element_type=F32))
    x1 = x_ref[...] + gate_ref[...] * mix
    x1_ref[...] = x1
    h2 = _rms(x1, g_ref[...]) * (1.0 + scale_ref[...]) + shift_ref[...]
    h2_ref[...] = h2.astype(h2_ref.dtype)
    if moe:
        logits = jnp.dot(h2.astype(BF16), wr_ref[...], preferred_element_type=F32)
        lane = lax.broadcasted_iota(jnp.int32, logits.shape, 1)
        logits = jnp.where(lane < n_experts, logits, -jnp.inf)
        m1 = jnp.max(logits, axis=-1, keepdims=True)
        i1 = jnp.min(jnp.where(logits == m1, lane, LANES), axis=-1, keepdims=True)
        rest = jnp.where(lane == i1, -jnp.inf, logits)
        m2 = jnp.max(rest, axis=-1, keepdims=True)
        i2 = jnp.min(jnp.where(rest == m2, lane, LANES), axis=-1, keepdims=True)
        e2 = jnp.exp(m2 - m1)
        w1 = 1.0 / (1.0 + e2)
        w2 = e2 / (1.0 + e2)
        col = lax.broadcasted_iota(jnp.int32, idx_ref.shape, 1)
        idx_ref[...] = jnp.where(col == 0, i1, i2)
        wts_ref[...] = jnp.where(col == 0, w1, w2)


def _out_proj(cv, att, x, gate, shift, scale, g, wo_b, wr_b, *, per_row_mod, n_experts):
    moe = wr_b is not None
    b, t_len, d = x.shape
    c = cv.shape[-1]
    tm = min(512, t_len)
    nt = t_len // tm
    row = lambda bi, ti: (bi, ti, 0)
    vec = lambda bi, ti: (0, 0)

    def mod_spec(i):
        if per_row_mod:
            return pl.BlockSpec((None, None, tm, d), lambda bi, ti: (i, bi, ti, 0))
        return pl.BlockSpec((None, None, 1, d), lambda bi, ti: (bi, i, 0, 0))

    in_specs = [
        pl.BlockSpec((None, tm, c), row), pl.BlockSpec((None, tm, c), row), pl.BlockSpec((None, tm, d), row),
        mod_spec(2), mod_spec(3), mod_spec(4),
        pl.BlockSpec((1, d), vec), pl.BlockSpec(wo_b.shape, vec),
    ]
    args = [cv, att, x, gate, shift, scale, g, wo_b]
    out_specs = [pl.BlockSpec((None, tm, d), row), pl.BlockSpec((None, tm, d), row)]
    out_shape = [jax.ShapeDtypeStruct((b, t_len, d), F32),
                 jax.ShapeDtypeStruct((b, t_len, d), BF16)]
    if moe:
        in_specs.append(pl.BlockSpec(wr_b.shape, vec))
        args.append(wr_b)
        out_specs += [pl.BlockSpec((None, tm, 2), row), pl.BlockSpec((None, tm, 2), row)]
        out_shape += [jax.ShapeDtypeStruct((b, t_len, 2), jnp.int32), jax.ShapeDtypeStruct((b, t_len, 2), F32)]
    return pl.pallas_call(
        functools.partial(_out_proj_kernel, moe=moe, n_experts=n_experts),
        grid=(b, nt),
        in_specs=in_specs,
        out_specs=out_specs,
        out_shape=out_shape,
        compiler_params=_cparams(("arbitrary", "arbitrary")),
        name="out_proj_router" if moe else "out_proj",
    )(*args)


def _dense_ffn_kernel(*refs, final):
    if final:
        h_ref, x1_ref, gate_ref, wg_ref, wu_ref, wd_ref, gf_ref, o_ref = refs
    else:
        h_ref, x1_ref, gate_ref, wg_ref, wu_ref, wd_ref, o_ref = refs
    hb = h_ref[...]
    gp = jnp.dot(hb, wg_ref[...], preferred_element_type=F32)
    up = jnp.dot(hb, wu_ref[...], preferred_element_type=F32)
    act = (_silu(gp) * up).astype(BF16)
    f = jnp.dot(act, wd_ref[...], preferred_element_type=F32)
    x = x1_ref[...] + gate_ref[...] * f
    o_ref[...] = _rms(x, gf_ref[...]) if final else x


def _dense_ffn(h2, x1, gate, wg_b, wu_b, wd_b, g_final, *, per_row_mod):
    b, t_len, d = x1.shape
    tm = min(512, t_len)
    nt = t_len // tm
    final = g_final is not None
    row = lambda bi, ti: (bi, ti, 0)
    vec = lambda bi, ti: (0, 0)
    if per_row_mod:
        gate_spec = pl.BlockSpec((None, None, tm, d), lambda bi, ti: (5, bi, ti, 0))
    else:
        gate_spec = pl.BlockSpec((None, None, 1, d), lambda bi, ti: (bi, 5, 0, 0))
    in_specs = [pl.BlockSpec((None, tm, d), row), pl.BlockSpec((None, tm, d), row), gate_spec,
                pl.BlockSpec(wg_b.shape, vec), pl.BlockSpec(wu_b.shape, vec), pl.BlockSpec(wd_b.shape, vec)]
    args = [h2, x1, gate, wg_b, wu_b, wd_b]
    if final:
        in_specs.append(pl.BlockSpec((1, d), vec))
        args.append(g_final)
    return pl.pallas_call(
        functools.partial(_dense_ffn_kernel, final=final),
        grid=(b, nt),
        in_specs=in_specs,
        out_specs=pl.BlockSpec((None, tm, d), row),
        out_shape=jax.ShapeDtypeStruct((b, t_len, d), F32),
        compiler_params=_cparams(("arbitrary", "arbitrary")),
        name="dense_ffn",
    )(*args)


def _moe_tile(n_pairs, n_experts):
    tile = 1024
    while tile > 128 and tile * n_experts > n_pairs:
        tile //= 2
    return tile


def _ff_chunk(dff):
    for c in (896, 512, 256, 128):
        if dff % c == 0:
            return c
    return dff


ROUTE_TOKENS = 512
SEG_ROWS = 160


def _route_tables(idx2, tm, n_experts):
    n = idx2.shape[0]
    ts = min(ROUTE_TOKENS, n)
    ntt = n // ts
    ar = jnp.arange(n_experts, dtype=jnp.int32)[None, :]
    member = ((idx2[:, 0:1] == ar) | (idx2[:, 1:2] == ar)).astype(jnp.int32)
    cnt = member.reshape(ntt, ts, n_experts).sum(axis=1)
    cnt8 = (cnt + SUBLANES - 1) // SUBLANES * SUBLANES
    total = cnt8.sum(axis=0)
    padded = (total + SEG_ROWS + tm - 1) // tm * tm
    gend = jnp.cumsum(padded)
    gstart = gend - padded
    off = gstart[None, :] + jnp.cumsum(cnt8, axis=0) - cnt8
    n_tiles = (2 * n + (SUBLANES - 1) * ntt * n_experts + n_experts * (SEG_ROWS + tm)) // tm + 1
    tile_first = jnp.arange(n_tiles, dtype=jnp.int32) * tm
    tile_valid = (tile_first < gend[-1]).astype(jnp.int32)
    tile_expert = jnp.minimum(jnp.searchsorted(gend, tile_first, side="right"), n_experts - 1).astype(jnp.int32)
    last_valid_expert = tile_expert[jnp.maximum(jnp.sum(tile_valid) - 1, 0)]
    tile_expert = jnp.where(tile_valid > 0, tile_expert, last_valid_expert)
    return (off.astype(jnp.int32).reshape(-1), cnt.astype(jnp.int32).reshape(-1), tile_expert, tile_valid,
            ts, ntt, n_tiles)


def _selection(idx_ref, n_experts, base):
    idx = idx_ref[...]
    ts = idx.shape[0]
    lane = lax.broadcasted_iota(jnp.int32, (ts, LANES), 1)
    member = jnp.logical_or(lane == idx[:, 0:1], lane == idx[:, 1:2])
    before = (lax.broadcasted_iota(jnp.int32, (ts, ts), 1) < lax.broadcasted_iota(jnp.int32, (ts, ts), 0))
    rank = jnp.dot(before.astype(BF16), member.astype(BF16), preferred_element_type=F32)
    col = lax.broadcasted_iota(jnp.int32, (1, n_experts * SEG_ROWS), 1)
    col_e = jnp.zeros_like(col)
    for e in range(1, n_experts):
        col_e = col_e + (col >= e * SEG_ROWS).astype(jnp.int32)
    col_r = col - col_e * SEG_ROWS + base
    sels = []
    for k in range(2):
        rk = jnp.sum(jnp.where(lane == idx[:, k:k + 1], rank, 0.0), axis=-1, keepdims=True).astype(jnp.int32)
        sels.append(jnp.logical_and(col_e == idx[:, k:k + 1], col_r == rk))
    return sels


def _moe_dispatch_kernel(off_ref, cnt_ref, h_ref, idx_ref, xs_in, xs_hbm, buf_ref, sems, *, n_experts):
    del xs_in
    i = pl.program_id(0)
    slot = lax.rem(i, 2)

    def seg_copy(e, s, base):
        row = pl.multiple_of(off_ref[i * n_experts + e] + base, SUBLANES)
        return pltpu.make_async_copy(buf_ref.at[s, pl.ds(e * SEG_ROWS, SEG_ROWS)],
                                     xs_hbm.at[pl.ds(row, SEG_ROWS)], sems.at[s])

    def compact(base):
        s0, s1 = _selection(idx_ref, n_experts, base)
        sel = jnp.logical_or(s0, s1).astype(BF16)
        return lax.dot_general(sel, h_ref[...], (((0,), (0,)), ((), ())), preferred_element_type=F32)

    def wait_tile(s):
        pltpu.make_async_copy(buf_ref.at[s], xs_hbm.at[pl.ds(0, n_experts * SEG_ROWS)], sems.at[s]).wait()

    for par in (0, 1):
        @pl.when(slot == par)
        def _(par=par):
            buf_ref[par] = compact(0)
            @pl.when(i > 0)
            def _():
                wait_tile(1 - par)
            for e in range(n_experts):
                seg_copy(e, par, 0).start()

    longest = cnt_ref[i * n_experts]
    for e in range(1, n_experts):
        longest = jnp.maximum(longest, cnt_ref[i * n_experts + e])
    for base in range(SEG_ROWS, idx_ref.shape[0], SEG_ROWS):
        @pl.when(longest > base)
        def _(base=base):
            buf_ref[2] = compact(base)
            for e in range(n_experts):
                @pl.when(cnt_ref[i * n_experts + e] > base)
                def _(e=e):
                    cp = seg_copy(e, 2, base)
                    cp.start()
                    cp.wait()

    @pl.when(i == pl.num_programs(0) - 1)
    def _():
        for par in (0, 1):
            @pl.when(slot == par)
            def _(par=par):
                wait_tile(par)


def _moe_dispatch(h2b, idx2, off, cnt, ts, ntt, rows, n_experts):
    n, d = h2b.shape
    grid_spec = pltpu.PrefetchScalarGridSpec(
        num_scalar_prefetch=2,
        grid=(ntt,),
        in_specs=[
            pl.BlockSpec((ts, d), lambda i, o, c: (i, 0)),
            pl.BlockSpec((ts, 2), lambda i, o, c: (i, 0)),
            pl.BlockSpec(memory_space=pl.ANY),
        ],
        out_specs=pl.BlockSpec(memory_space=pl.ANY),
        scratch_shapes=[pltpu.VMEM((3, n_experts * SEG_ROWS, d), F32), pltpu.SemaphoreType.DMA((3,))],
    )
    return pl.pallas_call(
        functools.partial(_moe_dispatch_kernel, n_experts=n_experts),
        grid_spec=grid_spec,
        out_shape=jax.ShapeDtypeStruct((rows, d), F32),
        input_output_aliases={4: 0},
        compiler_params=_cparams(("arbitrary",)),
        name="moe_dispatch",
    )(off, cnt, h2b, idx2, jnp.zeros((rows, d), F32))


def _moe_grouped_kernel(te_ref, tv_ref, x_ref, wg_ref, wu_ref, wd_ref, y_ref, xb_ref, acc_ref):
    i = pl.program_id(0)
    j = pl.program_id(1)
    nj = pl.num_programs(1)

    @pl.when(tv_ref[i] > 0)
    def _():
        @pl.when(j == 0)
        def _():
            xb_ref[...] = x_ref[...].astype(BF16)
            acc_ref[...] = jnp.zeros_like(acc_ref)

        xb = xb_ref[...]
        gp = jnp.dot(xb, wg_ref[...].astype(BF16), preferred_element_type=F32)
        up = jnp.dot(xb, wu_ref[...].astype(BF16), preferred_element_type=F32)
        act = (_silu(gp) * up).astype(BF16)
        acc_ref[...] += jnp.dot(act, wd_ref[...].astype(BF16), preferred_element_type=F32)

        @pl.when(j == nj - 1)
        def _():
            y_ref[...] = acc_ref[...]

    @pl.when(jnp.logical_and(tv_ref[i] <= 0, j == nj - 1))
    def _():
        y_ref[...] = jnp.zeros_like(y_ref)


def _moe_grouped(xs, tile_expert, tile_valid, wg, wu, wd, tm):
    rows, d = xs.shape
    n_experts, _, dff = wg.shape
    fc = _ff_chunk(dff)
    grid_spec = pltpu.PrefetchScalarGridSpec(
        num_scalar_prefetch=2,
        grid=(rows // tm, dff // fc),
        in_specs=[
            pl.BlockSpec((tm, d), lambda i, j, te, tv: (i, 0)),
            pl.BlockSpec((None, d, fc), lambda i, j, te, tv: (te[i], 0, j)),
            pl.BlockSpec((None, d, fc), lambda i, j, te, tv: (te[i], 0, j)),
            pl.BlockSpec((None, fc, d), lambda i, j, te, tv: (te[i], j, 0)),
        ],
        out_specs=pl.BlockSpec((tm, d), lambda i, j, te, tv: (i, 0)),
        scratch_shapes=[pltpu.VMEM((tm, d), BF16), pltpu.VMEM((tm, d), F32)],
    )
    return pl.pallas_call(
        _moe_grouped_kernel,
        grid_spec=grid_spec,
        out_shape=jax.ShapeDtypeStruct((rows, d), F32),
        compiler_params=_cparams(("arbitrary", "arbitrary")),
        name="moe_grouped",
    )(tile_expert, tile_valid, xs, wg, wu, wd)


def _moe_undispatch_kernel(*refs, n_experts, final):
    if final:
        off_ref, cnt_ref, idx_ref, wts_ref, x1_ref, gate_ref, gf_ref, y_hbm, o_ref, ybuf_ref, sems = refs
    else:
        off_ref, cnt_ref, idx_ref, wts_ref, x1_ref, gate_ref, y_hbm, o_ref, ybuf_ref, sems = refs
    i = pl.program_id(0)
    nt = pl.num_programs(0)
    slot = lax.rem(i, 2)
    ts = idx_ref.shape[0]

    def seg_copy(tile, e, s, base):
        row = pl.multiple_of(off_ref[tile * n_experts + e] + base, SUBLANES)
        return pltpu.make_async_copy(y_hbm.at[pl.ds(row, SEG_ROWS)],
                                     ybuf_ref.at[s, pl.ds(e * SEG_ROWS, SEG_ROWS)], sems.at[s])

    def fetch(tile, s):
        for e in range(n_experts):
            seg_copy(tile, e, s, 0).start()

    def wait_tile(s):
        pltpu.make_async_copy(y_hbm.at[pl.ds(0, n_experts * SEG_ROWS)], ybuf_ref.at[s], sems.at[s]).wait()

    def gather(s, base):
        s0, s1 = _selection(idx_ref, n_experts, base)
        sel = jnp.concatenate([s0.astype(BF16), s1.astype(BF16)], axis=0)
        z = jnp.dot(sel, ybuf_ref[s].astype(BF16), preferred_element_type=F32)
        w = wts_ref[...]
        return w[:, 0:1] * z[:ts] + w[:, 1:2] * z[ts:]

    longest = cnt_ref[i * n_experts]
    for e in range(1, n_experts):
        longest = jnp.maximum(longest, cnt_ref[i * n_experts + e])

    for par in (0, 1):
        @pl.when(slot == par)
        def _(par=par):
            @pl.when(i == 0)
            def _():
                fetch(0, 0)

            @pl.when(i + 1 < nt)
            def _():
                fetch(i + 1, 1 - par)

            wait_tile(par)
            x = x1_ref[...] + gate_ref[...] * gather(par, 0)
            o_ref[...] = x

            for base in range(SEG_ROWS, ts, SEG_ROWS):
                @pl.when(longest > base)
                def _(base=base):
                    ybuf_ref[par] = jnp.zeros((n_experts * SEG_ROWS, ybuf_ref.shape[-1]), F32)
                    for e in range(n_experts):
                        @pl.when(cnt_ref[i * n_experts + e] > base)
                        def _(e=e):
                            cp = seg_copy(i, e, par, base)
                            cp.start()
                            cp.wait()
                    o_ref[...] = o_ref[...] + gate_ref[...] * gather(par, base)

            if final:
                o_ref[...] = _rms(o_ref[...], gf_ref[...])


def _moe_undispatch(yexp, idx2, wts2, x1, gate, g_final, off, cnt, ts, n_experts):
    b, t_len, d = x1.shape
    per_b = t_len // ts
    final = g_final is not None
    in_specs = [
        pl.BlockSpec((ts, 2), lambda i, o, c: (i, 0)),
        pl.BlockSpec((ts, 2), lambda i, o, c: (i, 0)),
        pl.BlockSpec((None, ts, d), lambda i, o, c: (i // per_b, i % per_b, 0)),
        pl.BlockSpec((None, None, 1, d), lambda i, o, c: (i // per_b, 5, 0, 0)),
    ]
    args = [idx2, wts2, x1, gate]
    if final:
        in_specs.append(pl.BlockSpec((1, d), lambda i, o, c: (0, 0)))
        args.append(g_final)
    in_specs.append(pl.BlockSpec(memory_space=pl.ANY))
    args.append(yexp)
    grid_spec = pltpu.PrefetchScalarGridSpec(
        num_scalar_prefetch=2,
        grid=(b * per_b,),
        in_specs=in_specs,
        out_specs=pl.BlockSpec((None, ts, d), lambda i, o, c: (i // per_b, i % per_b, 0)),
        scratch_shapes=[pltpu.VMEM((2, n_experts * SEG_ROWS, d), F32), pltpu.SemaphoreType.DMA((2,))],
    )
    return pl.pallas_call(
        functools.partial(_moe_undispatch_kernel, n_experts=n_experts, final=final),
        grid_spec=grid_spec,
        out_shape=jax.ShapeDtypeStruct((b, t_len, d), F32),
        compiler_params=_cparams(("arbitrary",)),
        name="moe_undispatch",
    )(off, cnt, *args)


def _moe_routed(h2b, idx2, wts2, x1, gate, g_final, wg, wu, wd):
    n, d = h2b.shape
    n_experts = wg.shape[0]
    tm = _moe_tile(2 * n, n_experts)
    off, cnt, tile_expert, tile_valid, ts, ntt, n_tiles = _route_tables(idx2, tm, n_experts)
    xs = _moe_dispatch(h2b, idx2, off, cnt, ts, ntt, n_tiles * tm, n_experts)
    yexp = _moe_grouped(xs, tile_expert, tile_valid, wg, wu, wd, tm)
    return _moe_undispatch(yexp, idx2, wts2, x1, gate, g_final, off, cnt, ts, n_experts)


def _moe_small_kernel(*refs, final):
    if final:
        h_ref, idx_ref, wts_ref, x1_ref, gate_ref, wg_ref, wu_ref, wd_ref, gf_ref, o_ref, acc_ref = refs
    else:
        h_ref, idx_ref, wts_ref, x1_ref, gate_ref, wg_ref, wu_ref, wd_ref, o_ref, acc_ref = refs
    e = pl.program_id(0)
    j = pl.program_id(1)

    @pl.when((e == 0) & (j == 0))
    def _():
        acc_ref[...] = jnp.zeros_like(acc_ref)

    hb = h_ref[...].astype(BF16)
    gp = jnp.dot(hb, wg_ref[...].astype(BF16), preferred_element_type=F32)
    up = jnp.dot(hb, wu_ref[...].astype(BF16), preferred_element_type=F32)
    act = (_silu(gp) * up).astype(BF16)
    y = jnp.dot(act, wd_ref[...].astype(BF16), preferred_element_type=F32)
    idx = idx_ref[...]
    wts = wts_ref[...]
    ge = (jnp.where(idx[:, 0:1] == e, wts[:, 0:1], 0.0) + jnp.where(idx[:, 1:2] == e, wts[:, 1:2], 0.0))
    acc_ref[...] += ge * y

    @pl.when((e == pl.num_programs(0) - 1) & (j == pl.num_programs(1) - 1))
    def _():
        x = x1_ref[...] + gate_ref[...] * acc_ref[...]
        o_ref[...] = _rms(x, gf_ref[...]) if final else x


def _moe_small(h2, idx, wts, x1, gate_rows, wg, wu, wd, g_final):
    rws, d = x1.shape
    n_experts, _, dff = wg.shape
    fc = _ff_chunk(dff)
    final = g_final is not None
    full = lambda e, j: (0, 0)
    in_specs = [
        pl.BlockSpec((rws, d), full), pl.BlockSpec((rws, 2), full), pl.BlockSpec((rws, 2), full),
        pl.BlockSpec((rws, d), full),
        pl.BlockSpec((None, rws, d), lambda e, j: (5, 0, 0)),
        pl.BlockSpec((None, d, fc), lambda e, j: (e, 0, j)),
        pl.BlockSpec((None, d, fc), lambda e, j: (e, 0, j)),
        pl.BlockSpec((None, fc, d), lambda e, j: (e, j, 0)),
    ]
    args = [h2, idx, wts, x1, gate_rows, wg, wu, wd]
    if final:
        in_specs.append(pl.BlockSpec((1, d), full))
        args.append(g_final)
    return pl.pallas_call(
        functools.partial(_moe_small_kernel, final=final),
        grid=(n_experts, dff // fc),
        in_specs=in_specs,
        out_specs=pl.BlockSpec((rws, d), full),
        out_shape=jax.ShapeDtypeStruct((rws, d), F32),
        scratch_shapes=[pltpu.VMEM((rws, d), F32)],
        compiler_params=_cparams(("arbitrary", "arbitrary")),
        name="moe_small",
    )(*args)


def _rope_tables(pos, head_dim):
    half = head_dim // 2
    inv_freq = 1.0 / (ROPE_THETA ** (jnp.arange(half, dtype=F32) / half))
    ang = pos.astype(F32)[:, None] * inv_freq[None, :]
    cos = jnp.tile(jnp.cos(ang), (1, LANES // half))
    sin = jnp.tile(jnp.sin(ang), (1, LANES // half))
    lane = jnp.arange(LANES)
    sign = jnp.where((lane % head_dim) < half, -1.0, 1.0).astype(F32)
    return cos, sin * sign[None, :]


def kernel(x_prompt, x_sample, c_prompt, c_sample, cache_k, cache_v, state_conv, w_ada, b_ada, g_mix, g_ffn, w_in, w_dw, b_dw, g_conv_ln, b_conv_ln, lambda_q1, lambda_k1, lambda_q2, lambda_k2, g_subln, w_out, w_ffn_gate, w_ffn_up, w_ffn_down, w_router, w_moe_gate, w_moe_up, w_moe_down, g_final):
    depth, d, _ = w_in.shape
    bp, tp, _ = x_prompt.shape
    bs, ts, _ = x_sample.shape
    past = cache_k.shape[2]
    n_sub, head_dim = cache_k.shape[3], cache_k.shape[4]
    att_dim = n_sub * head_dim
    width, conv_dim = w_dw.shape[1], w_dw.shape[2]
    n_experts = w_router.shape[2]
    assert tp % Q_BLOCK == 0 and d % LANES == 0 and conv_dim % LANES == 0 and att_dim % LANES == 0
    assert 2 * head_dim == LANES and width - 1 <= CONV_HALO and n_experts <= LANES

    mod_all = _adaln(jnp.concatenate([c_prompt, c_sample], axis=0), w_ada, b_ada)
    cos_p, sin_p = _rope_tables(jnp.arange(tp), head_dim)
    cos_s, sin_s = _rope_tables(past + jnp.arange(ts), head_dim)
    hist_p = jnp.zeros((bp, CONV_HALO, conv_dim), F32)
    pad_rows = CONV_HALO - (width - 1)
    cache_k2 = cache_k.reshape(depth, bs, past, att_dim)
    cache_v2 = cache_v.reshape(depth, bs, past, att_dim)
    g_fin = g_final.reshape(1, d)

    xp, xs = x_prompt, x_sample
    outs = {k: [] for k in ("cp", "cs")}
    kp_all = jnp.zeros((depth, bp, tp, att_dim), F32)
    vp_all = jnp.zeros((depth, bp, tp, att_dim), F32)
    ks_all = jnp.zeros((depth, bs, ts, att_dim), F32)
    vs_all = jnp.zeros((depth, bs, ts, att_dim), F32)
    for l in range(depth):
        last = l == depth - 1
        moe = l % 2 == 1
        lam_init = 0.8 - 0.6 * math.exp(-0.3 * l)
        w_in_b = w_in[l].astype(BF16)
        wo_b = w_out[l].astype(BF16)
        gm = g_mix[l].reshape(1, d)
        gf = g_ffn[l].reshape(1, d)
        lam_args = [a[l].reshape(1, head_dim) for a in (lambda_q1, lambda_k1, lambda_q2, lambda_k2)]
        gsub = g_subln[l].reshape(1, 2 * head_dim)
        conv_args = (w_dw[l], b_dw[l].reshape(1, conv_dim), g_conv_ln[l].reshape(1, conv_dim),
                     b_conv_ln[l].reshape(1, conv_dim))
        mod_p = mod_all[l, :bp].reshape(bp, 6, 1, d)
        mod_s = mod_all[l, bp:].reshape(bs, 6, 1, d)
        mod_s_rows = jnp.broadcast_to(mod_all[l, bp:].reshape(bs, 1, 6, d), (bs, ts, 6, d))
        mod_s_rows = mod_s_rows.transpose(2, 0, 1, 3).reshape(6, 1, bs * ts, d)
        if moe:
            i = l // 2
            wr_b = jnp.pad(w_router[i], ((0, 0), (0, LANES - n_experts))).astype(BF16)
        else:
            i = l // 2
            wr_b = None
            wg_b, wu_b, wd_b = (w_ffn_gate[i].astype(BF16), w_ffn_up[i].astype(BF16), w_ffn_down[i].astype(BF16))

        kp_all, vp_all, cp, qb, kb, vt, cv = _mixer_in(xp, mod_p, gm, w_in_b, cos_p, sin_p, hist_p, *conv_args,
                                                       kp_all, vp_all, layer=l, head_dim=head_dim,
                                                       transpose_v=True)
        att = _attn_prompt(qb, kb, vt, *lam_args, gsub, head_dim=head_dim, lam_init=lam_init)
        res = _out_proj(cv, att, xp, mod_p, mod_p, mod_p, gf, wo_b, wr_b, per_row_mod=False, n_experts=n_experts)
        if moe:
            x1, h2, idx, wts = res
            xp = _moe_routed(h2.reshape(bp * tp, d), idx.reshape(bp * tp, 2), wts.reshape(bp * tp, 2), x1, mod_p,
                             g_fin if last else None, w_moe_gate[i], w_moe_up[i], w_moe_down[i])
        else:
            x1, h2 = res
            xp = _dense_ffn(h2, x1, mod_p, wg_b, wu_b, wd_b, g_fin if last else None, per_row_mod=False)
        outs["cp"].append(cp)

        hist_s = jnp.pad(state_conv[l], ((0, 0), (pad_rows, 0), (0, 0)))
        ks_all, vs_all, cs, qb, kb, vb, cv = _mixer_in(xs, mod_s, gm, w_in_b, cos_s, sin_s, hist_s, *conv_args,
                                                       ks_all, vs_all, layer=l, head_dim=head_dim,
                                                       transpose_v=False)
        att = _attn_sample(qb, kb, vb, cache_k2[l], cache_v2[l], *lam_args, gsub,
                           head_dim=head_dim, lam_init=lam_init)
        rows = bs * ts
        res = _out_proj(cv.reshape(1, rows, conv_dim), att.reshape(1, rows, att_dim), xs.reshape(1, rows, d),
                        mod_s_rows, mod_s_rows, mod_s_rows, gf, wo_b, wr_b, per_row_mod=True, n_experts=n_experts)
        if moe:
            x1, h2, idx, wts = res
            xs = _moe_small(h2.reshape(rows, d), idx.reshape(rows, 2), wts.reshape(rows, 2), x1.reshape(rows, d),
                            mod_s_rows.reshape(6, rows, d), w_moe_gate[i], w_moe_up[i], w_moe_down[i],
                            g_fin if last else None)
        else:
            x1, h2 = res
            xs = _dense_ffn(h2, x1, mod_s_rows, wg_b, wu_b, wd_b, g_fin if last else None, per_row_mod=True)
        xs = xs.reshape(bs, ts, d)
        outs["cs"].append(cs)

    return (xp, xs,
            kp_all.reshape(depth, bp, tp, n_sub, head_dim), vp_all.reshape(depth, bp, tp, n_sub // 2, 2 * head_dim),
            jnp.stack(outs["cp"]),
            ks_all.reshape(depth, bs, ts, n_sub, head_dim), vs_all.reshape(depth, bs, ts, n_sub // 2, 2 * head_dim),
            jnp.stack(outs["cs"]))
```

```python
import functools
import math

import jax
import jax.numpy as jnp
from jax import lax
from jax.experimental import pallas as pl
from jax.experimental.pallas import tpu as pltpu

F32 = jnp.float32
BF16 = jnp.bfloat16

EPS = 1e-6
CHUNK = 64
ROPE_THETA = 10000.0
NEG_INF = -1e30

LANES = 128
SUBLANES = 8
VMEM_LIMIT_BYTES = 56 * 1024 * 1024

CONV_HALO = 32
Q_BLOCK = 256
ADA_COLS = 1536
MIXER_PART_ROWS = 128
REDUCE_ROWS = 64
LOG2E = 1.4426950408889634


def _cparams(sem):
    return pltpu.CompilerParams(dimension_semantics=sem, vmem_limit_bytes=VMEM_LIMIT_BYTES)


def _sigmoid(x):
    return 1.0 / (1.0 + jnp.exp(-x))


def _silu(x):
    return x * _sigmoid(x)


def _rms(x, g):
    return x * lax.rsqrt(jnp.mean(x * x, axis=-1, keepdims=True) + EPS) * g


def _col_reduce(x, op):
    rows, cols = x.shape
    if rows > REDUCE_ROWS and rows % REDUCE_ROWS == 0:
        x = op(x.reshape(rows // REDUCE_ROWS, REDUCE_ROWS, cols), axis=0)
    return op(x, axis=0, keepdims=True)


def _adaln_kernel(c_ref, w_ref, b_ref, o_ref):
    s = _silu(c_ref[...]).astype(BF16)
    o_ref[...] = jnp.dot(s, w_ref[...].astype(BF16), preferred_element_type=F32) + b_ref[...]


def _adaln(c_all, w_ada, b_ada):
    depth, d, n = w_ada.shape
    rows = c_all.shape[0]
    nb = ADA_COLS if n % ADA_COLS == 0 else n
    return pl.pallas_call(
        _adaln_kernel,
        grid=(depth, n // nb),
        in_specs=[
            pl.BlockSpec((rows, d), lambda l, j: (0, 0)),
            pl.BlockSpec((None, d, nb), lambda l, j: (l, 0, j)),
            pl.BlockSpec((None, 1, nb), lambda l, j: (l, 0, j)),
        ],
        out_specs=pl.BlockSpec((None, rows, nb), lambda l, j: (l, 0, j)),
        out_shape=jax.ShapeDtypeStruct((depth, rows, n), F32),
        compiler_params=_cparams(("arbitrary", "arbitrary")),
        name="adaln",
    )(c_all, w_ada, b_ada.reshape(depth, 1, n))


def _mixer_in_kernel(x_ref, shift_ref, scale_ref, g_ref, w_ref, cos_ref, sin_ref, hist_ref,
                     wdw_ref, bdw_ref, gln_ref, bln_ref, k_all_ref, v_all_ref,
                     k_ref, v_ref, nc_ref, qb_ref, kb_ref, vx_ref, cv_ref,
                     ext_ref, craw_ref, z_ref, *, tm, conv_dim, att_dim, head_dim, width, att_scale,
                     transpose_v, row_chunk, part_rows):
    del k_all_ref, v_all_ref
    t = pl.program_id(1)
    nt = pl.num_programs(1)

    @pl.when(t == 0)
    def _():
        for s in range(SUBLANES):
            ext_ref[s, 0:CONV_HALO - s, :] = hist_ref[s:CONV_HALO, :]

    first_tap = CONV_HALO - (width - 1)
    lane = lax.broadcasted_iota(jnp.int32, (1, LANES), 1)
    first_half = (lane & (head_dim - 1)) < (head_dim // 2)
    n_parts = tm // part_rows

    def project(p):
        rows = slice(p * part_rows, (p + 1) * part_rows)
        h = _rms(x_ref[rows, :], g_ref[...]) * (1.0 + scale_ref[...]) + shift_ref[...]
        z_ref[rows, :] = jnp.dot(h.astype(BF16), w_ref[...], preferred_element_type=F32)

    def finish(p):
        lo = p * part_rows
        rows = slice(lo, lo + part_rows)
        u = z_ref[rows, 0:conv_dim] * _sigmoid(z_ref[rows, conv_dim:2 * conv_dim])
        for s in range(SUBLANES):
            ext_ref[s, CONV_HALO - s + lo:CONV_HALO - s + lo + part_rows, :] = u
        for r0 in range(lo, lo + part_rows, row_chunk):
            for lb in range(conv_dim // LANES):
                sl = slice(lb * LANES, (lb + 1) * LANES)
                acc = jnp.broadcast_to(bdw_ref[:, sl], (row_chunk, LANES))
                for w in range(width):
                    s = (first_tap + w) % SUBLANES
                    a0 = r0 + first_tap + w - s
                    acc = acc + ext_ref[s, a0:a0 + row_chunk, sl] * wdw_ref[w:w + 1, sl]
                craw_ref[r0:r0 + row_chunk, sl] = acc
        cr = craw_ref[rows, :]
        mu = jnp.mean(cr, axis=-1, keepdims=True)
        dlt = cr - mu
        var = jnp.mean(dlt * dlt, axis=-1, keepdims=True)
        y = dlt * lax.rsqrt(var + EPS) * gln_ref[...] + bln_ref[...]
        cv_ref[rows, :] = _silu(y).astype(BF16)

        cos = cos_ref[rows, :]
        sin = sin_ref[rows, :]

        def rope(xb):
            partner = jnp.where(first_half,
                                pltpu.roll(xb, LANES - head_dim // 2, 1),
                                pltpu.roll(xb, head_dim // 2, 1))
            return xb * cos + partner * sin

        o = 2 * conv_dim
        for lb in range(att_dim // LANES):
            sl = slice(lb * LANES, (lb + 1) * LANES)
            qblk = rope(z_ref[rows, o + lb * LANES:o + (lb + 1) * LANES])
            qb_ref[rows, sl] = (qblk * att_scale).astype(BF16)
            kblk = rope(z_ref[rows, o + att_dim + lb * LANES:o + att_dim + (lb + 1) * LANES])
            k_ref[rows, sl] = kblk
            kb_ref[rows, sl] = kblk.astype(BF16)
        v = z_ref[rows, o + 2 * att_dim:o + 3 * att_dim]
        v_ref[rows, :] = v
        if transpose_v:
            vx_ref[:, rows] = v.T.astype(BF16)
        else:
            vx_ref[rows, :] = v.astype(BF16)

    project(0)
    for p in range(n_parts):
        if p + 1 < n_parts:
            project(p + 1)
        finish(p)

    @pl.when(t == nt - 1)
    def _():
        nc_ref[...] = ext_ref[0, tm + first_tap:tm + CONV_HALO, :]

    for s in range(SUBLANES):
        tail = ext_ref[s, tm:tm + CONV_HALO - s, :]
        ext_ref[s, 0:CONV_HALO - s, :] = tail


def _mixer_in(x, mod, g, w_in_b, cos_t, sin_t, hist, w_dw, b_dw, g_ln, b_ln, k_all, v_all, *, layer, head_dim,
              transpose_v):
    b, t_len, d = x.shape
    conv_dim = w_dw.shape[1]
    width = w_dw.shape[0]
    att_dim = (w_in_b.shape[1] - 2 * conv_dim) // 3
    tm = min(512, t_len)
    row_chunk = min(32, tm)
    nt = t_len // tm
    kern = functools.partial(
        _mixer_in_kernel, tm=tm, conv_dim=conv_dim, att_dim=att_dim, head_dim=head_dim, width=width,
        att_scale=head_dim ** -0.5 * LOG2E, transpose_v=transpose_v, row_chunk=row_chunk,
        part_rows=min(MIXER_PART_ROWS, tm))
    row = lambda bi, ti: (bi, ti, 0)
    vec = lambda bi, ti: (0, 0)
    if transpose_v:
        vx_spec = pl.BlockSpec((None, att_dim, tm), lambda bi, ti: (bi, 0, ti))
        vx_shape = jax.ShapeDtypeStruct((b, att_dim, t_len), BF16)
    else:
        vx_spec = pl.BlockSpec((None, tm, att_dim), row)
        vx_shape = jax.ShapeDtypeStruct((b, t_len, att_dim), BF16)
    return pl.pallas_call(
        kern,
        grid=(b, nt),
        in_specs=[
            pl.BlockSpec((None, tm, d), row),
            pl.BlockSpec((None, None, 1, d), lambda bi, ti: (bi, 0, 0, 0)),
            pl.BlockSpec((None, None, 1, d), lambda bi, ti: (bi, 1, 0, 0)),
            pl.BlockSpec((1, d), vec),
            pl.BlockSpec(w_in_b.shape, vec),
            pl.BlockSpec((tm, LANES), lambda bi, ti: (ti, 0)),
            pl.BlockSpec((tm, LANES), lambda bi, ti: (ti, 0)),
            pl.BlockSpec((None, CONV_HALO, conv_dim), lambda bi, ti: (bi, 0, 0)),
            pl.BlockSpec((width, conv_dim), vec),
            pl.BlockSpec((1, conv_dim), vec),
            pl.BlockSpec((1, conv_dim), vec),
            pl.BlockSpec((1, conv_dim), vec),
            pl.BlockSpec(memory_space=pl.ANY),
            pl.BlockSpec(memory_space=pl.ANY),
        ],
        out_specs=[
            pl.BlockSpec((None, None, tm, att_dim), lambda bi, ti: (layer, bi, ti, 0)),
            pl.BlockSpec((None, None, tm, att_dim), lambda bi, ti: (layer, bi, ti, 0)),
            pl.BlockSpec((None, width - 1, conv_dim), lambda bi, ti: (bi, 0, 0)),
            pl.BlockSpec((None, tm, att_dim), row),
            pl.BlockSpec((None, tm, att_dim), row),
            vx_spec,
            pl.BlockSpec((None, tm, conv_dim), row),
        ],
        out_shape=[
            jax.ShapeDtypeStruct(k_all.shape, F32),
            jax.ShapeDtypeStruct(v_all.shape, F32),
            jax.ShapeDtypeStruct((b, width - 1, conv_dim), F32),
            jax.ShapeDtypeStruct((b, t_len, att_dim), BF16),
            jax.ShapeDtypeStruct((b, t_len, att_dim), BF16),
            vx_shape,
            jax.ShapeDtypeStruct((b, t_len, conv_dim), BF16),
        ],
        scratch_shapes=[
            pltpu.VMEM((SUBLANES, CONV_HALO + tm, conv_dim), F32),
            pltpu.VMEM((tm, conv_dim), F32),
            pltpu.VMEM((tm, w_in_b.shape[1]), F32),
        ],
        input_output_aliases={12: 0, 13: 1},
        compiler_params=_cparams(("arbitrary", "arbitrary")),
        name="mixer_in",
    )(x, mod, mod, g, w_in_b, cos_t, sin_t, hist, w_dw, b_dw, g_ln, b_ln, k_all, v_all)


def _lambda(lq1_ref, lk1_ref, lq2_ref, lk2_ref, lam_init):
    a = jnp.sum(lq1_ref[...] * lk1_ref[...], axis=-1, keepdims=True)
    b = jnp.sum(lq2_ref[...] * lk2_ref[...], axis=-1, keepdims=True)
    return jnp.exp(a) - jnp.exp(b) + lam_init


def _attn_prompt_kernel(q_ref, k_ref, vt_ref, lq1_ref, lk1_ref, lq2_ref, lk2_ref, gsub_ref, o_ref,
                        s_ref, e_ref, *, t_len, head_dim, lam_init):
    lam = _lambda(lq1_ref, lk1_ref, lq2_ref, lk2_ref, lam_init)
    n_pairs = q_ref.shape[-1] // LANES
    lane = lax.broadcasted_iota(jnp.int32, (1, LANES), 1)
    low = lane < head_dim
    kc = lax.broadcasted_iota(jnp.int32, (Q_BLOCK, Q_BLOCK), 0) // CHUNK
    qc = lax.broadcasted_iota(jnp.int32, (Q_BLOCK, Q_BLOCK), 1) // CHUNK
    visible = kc <= qc
    nt_dims = (((1,), (1,)), ((), ()))

    units = [(j, hp, sub) for j in range(t_len // Q_BLOCK) for hp in range(n_pairs) for sub in range(2)]

    def scores(n):
        j, hp, sub = units[n]
        tk = (j + 1) * Q_BLOCK
        sl = slice(hp * LANES, (hp + 1) * LANES)
        qp = q_ref[j * Q_BLOCK:tk, sl]
        qm = jnp.where(low if sub == 0 else jnp.logical_not(low), qp, jnp.zeros_like(qp))
        s = lax.dot_general(k_ref[0:tk, sl], qm, nt_dims, preferred_element_type=F32)
        diag = jnp.where(visible, s[j * Q_BLOCK:, :], NEG_INF)
        s = diag if j == 0 else jnp.concatenate([s[:j * Q_BLOCK, :], diag], axis=0)
        s_ref[n % 2, 0:tk, :] = s
        return _col_reduce(s, jnp.max)

    def probs(n, m):
        tk = (units[n][0] + 1) * Q_BLOCK
        l = None
        for r0 in range(0, tk, Q_BLOCK):
            e = jnp.exp2(s_ref[n % 2, r0:r0 + Q_BLOCK, :] - m)
            e_ref[n % 2, r0:r0 + Q_BLOCK, :] = e.astype(BF16)
            part = jnp.sum(e.reshape(Q_BLOCK // REDUCE_ROWS, REDUCE_ROWS, Q_BLOCK), axis=0)
            l = part if l is None else l + part
        return jnp.sum(l, axis=0, keepdims=True)

    def values(n):
        j, hp, _ = units[n]
        tk = (j + 1) * Q_BLOCK
        sl = slice(hp * LANES, (hp + 1) * LANES)
        return jnp.dot(vt_ref[sl, 0:tk], e_ref[n % 2, 0:tk, :], preferred_element_type=F32)

    maxes, sums, outs = {}, {}, {}
    for step in range(len(units) + 2):
        if step < len(units):
            maxes[step] = scores(step)
        if 0 <= step - 1 < len(units):
            sums[step - 1] = probs(step - 1, maxes.pop(step - 1))
        n = step - 2
        if 0 <= n < len(units):
            outs[n] = values(n)
            j, hp, sub = units[n]
            if sub == 1:
                ot = outs.pop(n - 1) * (1.0 / sums.pop(n - 1)) - outs.pop(n) * (lam / sums.pop(n))
                ms = jnp.mean(ot * ot, axis=0, keepdims=True)
                on = (ot * lax.rsqrt(ms + EPS)).T * (gsub_ref[...] * (1.0 - lam_init))
                o_ref[j * Q_BLOCK:(j + 1) * Q_BLOCK, hp * LANES:(hp + 1) * LANES] = on.astype(BF16)


def _attn_prompt(qb, kb, vt, lq1, lk1, lq2, lk2, gsub, *, head_dim, lam_init):
    b, t_len, a = qb.shape
    kern = functools.partial(_attn_prompt_kernel, t_len=t_len, head_dim=head_dim, lam_init=lam_init)
    vec = lambda bi: (0, 0)
    return pl.pallas_call(
        kern,
        grid=(b,),
        in_specs=[
            pl.BlockSpec((None, t_len, a), lambda bi: (bi, 0, 0)),
            pl.BlockSpec((None, t_len, a), lambda bi: (bi, 0, 0)),
            pl.BlockSpec((None, a, t_len), lambda bi: (bi, 0, 0)),
            pl.BlockSpec((1, head_dim), vec), pl.BlockSpec((1, head_dim), vec),
            pl.BlockSpec((1, head_dim), vec), pl.BlockSpec((1, head_dim), vec),
            pl.BlockSpec((1, 2 * head_dim), vec),
        ],
        out_specs=pl.BlockSpec((None, t_len, a), lambda bi: (bi, 0, 0)),
        out_shape=jax.ShapeDtypeStruct((b, t_len, a), BF16),
        scratch_shapes=[pltpu.VMEM((2, t_len, Q_BLOCK), F32), pltpu.VMEM((2, t_len, Q_BLOCK), BF16)],
        compiler_params=_cparams(("arbitrary",)),
        name="attn_prompt",
    )(qb, kb, vt, lq1, lk1, lq2, lk2, gsub)


def _attn_sample_kernel(q_ref, kn_ref, vn_ref, kc_ref, vc_ref, lq1_ref, lk1_ref, lq2_ref, lk2_ref, gsub_ref,
                        o_ref, *, head_dim, lam_init):
    lam = _lambda(lq1_ref, lk1_ref, lq2_ref, lk2_ref, lam_init)
    n_pairs = q_ref.shape[-1] // LANES
    lane = lax.broadcasted_iota(jnp.int32, (1, LANES), 1)
    low = lane < head_dim
    nt_dims = (((1,), (1,)), ((), ()))
    for hp in range(n_pairs):
        sl = slice(hp * LANES, (hp + 1) * LANES)
        qp = q_ref[:, sl]
        kc = kc_ref[:, sl].astype(BF16)
        vc = vc_ref[:, sl].astype(BF16)
        kn = kn_ref[:, sl]
        vn = vn_ref[:, sl]
        parts = []
        for sub in range(2):
            qm = jnp.where(low if sub == 0 else jnp.logical_not(low), qp, jnp.zeros_like(qp))
            sc = lax.dot_general(qm, kc, nt_dims, preferred_element_type=F32)
            sn = lax.dot_general(qm, kn, nt_dims, preferred_element_type=F32)
            m = jnp.maximum(jnp.max(sc, axis=-1, keepdims=True), jnp.max(sn, axis=-1, keepdims=True))
            ec = jnp.exp2(sc - m)
            en = jnp.exp2(sn - m)
            l = jnp.sum(ec, axis=-1, keepdims=True) + jnp.sum(en, axis=-1, keepdims=True)
            o = (jnp.dot(ec.astype(BF16), vc, preferred_element_type=F32)
                 + jnp.dot(en.astype(BF16), vn, preferred_element_type=F32))
            parts.append((o, l))
        oh = parts[0][0] * (1.0 / parts[0][1]) - parts[1][0] * (lam / parts[1][1])
        on = _rms(oh, gsub_ref[...]) * (1.0 - lam_init)
        o_ref[:, sl] = on.astype(BF16)


def _attn_sample(qb, kb, vb, cache_k, cache_v, lq1, lk1, lq2, lk2, gsub, *, head_dim, lam_init):
    b, tq, a = qb.shape
    past = cache_k.shape[1]
    kern = functools.partial(_attn_sample_kernel, head_dim=head_dim, lam_init=lam_init)
    vec = lambda bi: (0, 0)
    row = lambda bi: (bi, 0, 0)
    return pl.pallas_call(
        kern,
        grid=(b,),
        in_specs=[
            pl.BlockSpec((None, tq, a), row), pl.BlockSpec((None, tq, a), row), pl.BlockSpec((None, tq, a), row),
            pl.BlockSpec((None, past, a), row), pl.BlockSpec((None, past, a), row),
            pl.BlockSpec((1, head_dim), vec), pl.BlockSpec((1, head_dim), vec),
            pl.BlockSpec((1, head_dim), vec), pl.BlockSpec((1, head_dim), vec),
            pl.BlockSpec((1, 2 * head_dim), vec),
        ],
        out_specs=pl.BlockSpec((None, tq, a), row),
        out_shape=jax.ShapeDtypeStruct((b, tq, a), BF16),
        compiler_params=_cparams(("arbitrary",)),
        name="attn_sample",
    )(qb, kb, vb, cache_k, cache_v, lq1, lk1, lq2, lk2, gsub)


def _out_proj_kernel(*refs, moe, n_experts):
    if moe:
        (cv_ref, att_ref, x_ref, gate_ref, shift_ref, scale_ref, g_ref, wo_ref, wr_ref,
         x1_ref, h2_ref, idx_ref, wts_ref) = refs
    else:
        cv_ref, att_ref, x_ref, gate_ref, shift_ref, scale_ref, g_ref, wo_ref, x1_ref, h2_ref = refs
    c = cv_ref.shape[-1]
    mix = (jnp.dot(cv_ref[...], wo_ref[0:c, :], preferred_element_type=F32)
           + jnp.dot(att_ref[...], wo_ref[c:, :], preferred_element_type=F32))
    x1 = x_ref[...] + gate_ref[...] * mix
    x1_ref[...] = x1
    h2 = _rms(x1, g_ref[...]) * (1.0 + scale_ref[...]) + shift_ref[...]
    h2_ref[...] = h2.astype(h2_ref.dtype)
    if moe:
        logits = jnp.dot(h2.astype(BF16), wr_ref[...], preferred_element_type=F32)
        lane = lax.broadcasted_iota(jnp.int32, logits.shape, 1)
        logits = jnp.where(lane < n_experts, logits, -jnp.inf)
        m1 = jnp.max(logits, axis=-1, keepdims=True)
        i1 = jnp.min(jnp.where(logits == m1, lane, LANES), axis=-1, keepdims=True)
        rest = jnp.where(lane == i1, -jnp.inf, logits)
        m2 = jnp.max(rest, axis=-1, keepdims=True)
        i2 = jnp.min(jnp.where(rest == m2, lane, LANES), axis=-1, keepdims=True)
        e2 = jnp.exp(m2 - m1)
        w1 = 1.0 / (1.0 + e2)
        w2 = e2 / (1.0 + e2)
        col = lax.broadcasted_iota(jnp.int32, idx_ref.shape, 1)
        idx_ref[...] = jnp.where(col == 0, i1, i2)
        wts_ref[...] = jnp.where(col == 0, w1, w2)


def _out_proj(cv, att, x, gate, shift, scale, g, wo_b, wr_b, *, per_row_mod, n_experts):
    moe = wr_b is not None
    b, t_len, d = x.shape
    c = cv.shape[-1]
    tm = min(512, t_len)
    nt = t_len // tm
    row = lambda bi, ti: (bi, ti, 0)
    vec = lambda bi, ti: (0, 0)

    def mod_spec(i):
        if per_row_mod:
            return pl.BlockSpec((None, None, tm, d), lambda bi, ti: (i, bi, ti, 0))
        return pl.BlockSpec((None, None, 1, d), lambda bi, ti: (bi, i, 0, 0))

    in_specs = [
        pl.BlockSpec((None, tm, c), row), pl.BlockSpec((None, tm, c), row), pl.BlockSpec((None, tm, d), row),
        mod_spec(2), mod_spec(3), mod_spec(4),
        pl.BlockSpec((1, d), vec), pl.BlockSpec(wo_b.shape, vec),
    ]
    args = [cv, att, x, gate, shift, scale, g, wo_b]
    out_specs = [pl.BlockSpec((None, tm, d), row), pl.BlockSpec((None, tm, d), row)]
    out_shape = [jax.ShapeDtypeStruct((b, t_len, d), F32),
                 jax.ShapeDtypeStruct((b, t_len, d), BF16)]
    if moe:
        in_specs.append(pl.BlockSpec(wr_b.shape, vec))
        args.append(wr_b)
        out_specs += [pl.BlockSpec((None, tm, 2), row), pl.BlockSpec((None, tm, 2), row)]
        out_shape += [jax.ShapeDtypeStruct((b, t_len, 2), jnp.int32), jax.ShapeDtypeStruct((b, t_len, 2), F32)]
    return pl.pallas_call(
        functools.partial(_out_proj_kernel, moe=moe, n_experts=n_experts),
        grid=(b, nt),
        in_specs=in_specs,
        out_specs=out_specs,
        out_shape=out_shape,
        compiler_params=_cparams(("arbitrary", "arbitrary")),
        name="out_proj_router" if moe else "out_proj",
    )(*args)


def _dense_ffn_kernel(*refs, final):
    if final:
        h_ref, x1_ref, gate_ref, wg_ref, wu_ref, wd_ref, gf_ref, o_ref = refs
    else:
        h_ref, x1_ref, gate_ref, wg_ref, wu_ref, wd_ref, o_ref = refs
    hb = h_ref[...]
    gp = jnp.dot(hb, wg_ref[...], preferred_element_type=F32)
    up = jnp.dot(hb, wu_ref[...], preferred_element_type=F32)
    act = (_silu(gp) * up).astype(BF16)
    f = jnp.dot(act, wd_ref[...], preferred_element_type=F32)
    x = x1_ref[...] + gate_ref[...] * f
    o_ref[...] = _rms(x, gf_ref[...]) if final else x


def _dense_ffn(h2, x1, gate, wg_b, wu_b, wd_b, g_final, *, per_row_mod):
    b, t_len, d = x1.shape
    tm = min(512, t_len)
    nt = t_len // tm
    final = g_final is not None
    row = lambda bi, ti: (bi, ti, 0)
    vec = lambda bi, ti: (0, 0)
    if per_row_mod:
        gate_spec = pl.BlockSpec((None, None, tm, d), lambda bi, ti: (5, bi, ti, 0))
    else:
        gate_spec = pl.BlockSpec((None, None, 1, d), lambda bi, ti: (bi, 5, 0, 0))
    in_specs = [pl.BlockSpec((None, tm, d), row), pl.BlockSpec((None, tm, d), row), gate_spec,
                pl.BlockSpec(wg_b.shape, vec), pl.BlockSpec(wu_b.shape, vec), pl.BlockSpec(wd_b.shape, vec)]
    args = [h2, x1, gate, wg_b, wu_b, wd_b]
    if final:
        in_specs.append(pl.BlockSpec((1, d), vec))
        args.append(g_final)
    return pl.pallas_call(
        functools.partial(_dense_ffn_kernel, final=final),
        grid=(b, nt),
        in_specs=in_specs,
        out_specs=pl.BlockSpec((None, tm, d), row),
        out_shape=jax.ShapeDtypeStruct((b, t_len, d), F32),
        compiler_params=_cparams(("arbitrary", "arbitrary")),
        name="dense_ffn",
    )(*args)


def _moe_tile(n_pairs, n_experts):
    tile = 1024
    while tile > 128 and tile * n_experts > n_pairs:
        tile //= 2
    return tile


def _ff_chunk(dff):
    for c in (512, 256, 128):
        if dff % c == 0:
            return c
    return dff


ROUTE_TOKENS = 256
SEG_ROWS = 128


def _route_tables(idx2, tm, n_experts):
    n = idx2.shape[0]
    ts = min(ROUTE_TOKENS, n)
    ntt = n // ts
    ar = jnp.arange(n_experts, dtype=jnp.int32)[None, :]
    member = ((idx2[:, 0:1] == ar) | (idx2[:, 1:2] == ar)).astype(jnp.int32)
    cnt = member.reshape(ntt, ts, n_experts).sum(axis=1)
    cnt8 = (cnt + SUBLANES - 1) // SUBLANES * SUBLANES
    total = cnt8.sum(axis=0)
    padded = (total + SEG_ROWS + tm - 1) // tm * tm
    gend = jnp.cumsum(padded)
    gstart = gend - padded
    off = gstart[None, :] + jnp.cumsum(cnt8, axis=0) - cnt8
    n_tiles = (2 * n + (SUBLANES - 1) * ntt * n_experts + n_experts * (SEG_ROWS + tm)) // tm + 1
    tile_first = jnp.arange(n_tiles, dtype=jnp.int32) * tm
    tile_valid = (tile_first < gend[-1]).astype(jnp.int32)
    tile_expert = jnp.minimum(jnp.searchsorted(gend, tile_first, side="right"), n_experts - 1).astype(jnp.int32)
    last_valid_expert = tile_expert[jnp.maximum(jnp.sum(tile_valid) - 1, 0)]
    tile_expert = jnp.where(tile_valid > 0, tile_expert, last_valid_expert)
    return (off.astype(jnp.int32).reshape(-1), cnt.astype(jnp.int32).reshape(-1), tile_expert, tile_valid,
            ts, ntt, n_tiles)


def _selection(idx_ref, n_experts, base):
    idx = idx_ref[...]
    ts = idx.shape[0]
    lane = lax.broadcasted_iota(jnp.int32, (ts, LANES), 1)
    member = jnp.logical_or(lane == idx[:, 0:1], lane == idx[:, 1:2])
    before = (lax.broadcasted_iota(jnp.int32, (ts, ts), 1) < lax.broadcasted_iota(jnp.int32, (ts, ts), 0))
    rank = jnp.dot(before.astype(BF16), member.astype(BF16), preferred_element_type=F32)
    col = lax.broadcasted_iota(jnp.int32, (1, n_experts * SEG_ROWS), 1)
    col_e = jnp.zeros_like(col)
    for e in range(1, n_experts):
        col_e = col_e + (col >= e * SEG_ROWS).astype(jnp.int32)
    col_r = col - col_e * SEG_ROWS + base
    sels = []
    for k in range(2):
        rk = jnp.sum(jnp.where(lane == idx[:, k:k + 1], rank, 0.0), axis=-1, keepdims=True).astype(jnp.int32)
        sels.append(jnp.logical_and(col_e == idx[:, k:k + 1], col_r == rk))
    return sels


def _moe_dispatch_kernel(off_ref, cnt_ref, h_ref, idx_ref, xs_in, xs_hbm, buf_ref, sems, *, n_experts):
    del xs_in
    i = pl.program_id(0)
    slot = lax.rem(i, 2)

    def seg_copy(e, s, base):
        row = pl.multiple_of(off_ref[i * n_experts + e] + base, SUBLANES)
        return pltpu.make_async_copy(buf_ref.at[s, pl.ds(e * SEG_ROWS, SEG_ROWS)],
                                     xs_hbm.at[pl.ds(row, SEG_ROWS)], sems.at[s])

    def compact(base):
        s0, s1 = _selection(idx_ref, n_experts, base)
        sel = jnp.logical_or(s0, s1).astype(BF16)
        return lax.dot_general(sel, h_ref[...], (((0,), (0,)), ((), ())), preferred_element_type=F32)

    def wait_tile(s):
        pltpu.make_async_copy(buf_ref.at[s], xs_hbm.at[pl.ds(0, n_experts * SEG_ROWS)], sems.at[s]).wait()

    for par in (0, 1):
        @pl.when(slot == par)
        def _(par=par):
            buf_ref[par] = compact(0)
            @pl.when(i > 0)
            def _():
                wait_tile(1 - par)
            for e in range(n_experts):
                seg_copy(e, par, 0).start()

    longest = cnt_ref[i * n_experts]
    for e in range(1, n_experts):
        longest = jnp.maximum(longest, cnt_ref[i * n_experts + e])
    for base in range(SEG_ROWS, idx_ref.shape[0], SEG_ROWS):
        @pl.when(longest > base)
        def _(base=base):
            buf_ref[2] = compact(base)
            for e in range(n_experts):
                @pl.when(cnt_ref[i * n_experts + e] > base)
                def _(e=e):
                    cp = seg_copy(e, 2, base)
                    cp.start()
                    cp.wait()

    @pl.when(i == pl.num_programs(0) - 1)
    def _():
        for par in (0, 1):
            @pl.when(slot == par)
            def _(par=par):
                wait_tile(par)


def _moe_dispatch(h2b, idx2, off, cnt, ts, ntt, rows, n_experts):
    n, d = h2b.shape
    grid_spec = pltpu.PrefetchScalarGridSpec(
        num_scalar_prefetch=2,
        grid=(ntt,),
        in_specs=[
            pl.BlockSpec((ts, d), lambda i, o, c: (i, 0)),
            pl.BlockSpec((ts, 2), lambda i, o, c: (i, 0)),
            pl.BlockSpec(memory_space=pl.ANY),
        ],
        out_specs=pl.BlockSpec(memory_space=pl.ANY),
        scratch_shapes=[pltpu.VMEM((3, n_experts * SEG_ROWS, d), F32), pltpu.SemaphoreType.DMA((3,))],
    )
    return pl.pallas_call(
        functools.partial(_moe_dispatch_kernel, n_experts=n_experts),
        grid_spec=grid_spec,
        out_shape=jax.ShapeDtypeStruct((rows, d), F32),
        input_output_aliases={4: 0},
        compiler_params=_cparams(("arbitrary",)),
        name="moe_dispatch",
    )(off, cnt, h2b, idx2, jnp.zeros((rows, d), F32))


def _moe_grouped_kernel(te_ref, tv_ref, x_ref, wg_ref, wu_ref, wd_ref, y_ref, xb_ref, acc_ref):
    i = pl.program_id(0)
    j = pl.program_id(1)
    nj = pl.num_programs(1)

    @pl.when(tv_ref[i] > 0)
    def _():
        @pl.when(j == 0)
        def _():
            xb_ref[...] = x_ref[...].astype(BF16)
            acc_ref[...] = jnp.zeros_like(acc_ref)

        xb = xb_ref[...]
        gp = jnp.dot(xb, wg_ref[...].astype(BF16), preferred_element_type=F32)
        up = jnp.dot(xb, wu_ref[...].astype(BF16), preferred_element_type=F32)
        act = (_silu(gp) * up).astype(BF16)
        acc_ref[...] += jnp.dot(act, wd_ref[...].astype(BF16), preferred_element_type=F32)

        @pl.when(j == nj - 1)
        def _():
            y_ref[...] = acc_ref[...]

    @pl.when(jnp.logical_and(tv_ref[i] <= 0, j == nj - 1))
    def _():
        y_ref[...] = jnp.zeros_like(y_ref)


def _moe_grouped(xs, tile_expert, tile_valid, wg, wu, wd, tm):
    rows, d = xs.shape
    n_experts, _, dff = wg.shape
    fc = _ff_chunk(dff)
    grid_spec = pltpu.PrefetchScalarGridSpec(
        num_scalar_prefetch=2,
        grid=(rows // tm, dff // fc),
        in_specs=[
            pl.BlockSpec((tm, d), lambda i, j, te, tv: (i, 0)),
            pl.BlockSpec((None, d, fc), lambda i, j, te, tv: (te[i], 0, j)),
            pl.BlockSpec((None, d, fc), lambda i, j, te, tv: (te[i], 0, j)),
            pl.BlockSpec((None, fc, d), lambda i, j, te, tv: (te[i], j, 0)),
        ],
        out_specs=pl.BlockSpec((tm, d), lambda i, j, te, tv: (i, 0)),
        scratch_shapes=[pltpu.VMEM((tm, d), BF16), pltpu.VMEM((tm, d), F32)],
    )
    return pl.pallas_call(
        _moe_grouped_kernel,
        grid_spec=grid_spec,
        out_shape=jax.ShapeDtypeStruct((rows, d), F32),
        compiler_params=_cparams(("arbitrary", "arbitrary")),
        name="moe_grouped",
    )(tile_expert, tile_valid, xs, wg, wu, wd)


def _moe_undispatch_kernel(*refs, n_experts, final):
    if final:
        off_ref, cnt_ref, idx_ref, wts_ref, x1_ref, gate_ref, gf_ref, y_hbm, o_ref, ybuf_ref, sems = refs
    else:
        off_ref, cnt_ref, idx_ref, wts_ref, x1_ref, gate_ref, y_hbm, o_ref, ybuf_ref, sems = refs
    i = pl.program_id(0)
    nt = pl.num_programs(0)
    slot = lax.rem(i, 2)
    ts = idx_ref.shape[0]

    def seg_copy(tile, e, s, base):
        row = pl.multiple_of(off_ref[tile * n_experts + e] + base, SUBLANES)
        return pltpu.make_async_copy(y_hbm.at[pl.ds(row, SEG_ROWS)],
                                     ybuf_ref.at[s, pl.ds(e * SEG_ROWS, SEG_ROWS)], sems.at[s])

    def fetch(tile, s):
        for e in range(n_experts):
            seg_copy(tile, e, s, 0).start()

    def wait_tile(s):
        pltpu.make_async_copy(y_hbm.at[pl.ds(0, n_experts * SEG_ROWS)], ybuf_ref.at[s], sems.at[s]).wait()

    def gather(s, base):
        s0, s1 = _selection(idx_ref, n_experts, base)
        sel = jnp.concatenate([s0.astype(BF16), s1.astype(BF16)], axis=0)
        z = jnp.dot(sel, ybuf_ref[s].astype(BF16), preferred_element_type=F32)
        w = wts_ref[...]
        return w[:, 0:1] * z[:ts] + w[:, 1:2] * z[ts:]

    longest = cnt_ref[i * n_experts]
    for e in range(1, n_experts):
        longest = jnp.maximum(longest, cnt_ref[i * n_experts + e])

    for par in (0, 1):
        @pl.when(slot == par)
        def _(par=par):
            @pl.when(i == 0)
            def _():
                fetch(0, 0)

            @pl.when(i + 1 < nt)
            def _():
                fetch(i + 1, 1 - par)

            wait_tile(par)
            x = x1_ref[...] + gate_ref[...] * gather(par, 0)
            o_ref[...] = x

            for base in range(SEG_ROWS, ts, SEG_ROWS):
                @pl.when(longest > base)
                def _(base=base):
                    ybuf_ref[par] = jnp.zeros((n_experts * SEG_ROWS, ybuf_ref.shape[-1]), F32)
                    for e in range(n_experts):
                        @pl.when(cnt_ref[i * n_experts + e] > base)
                        def _(e=e):
                            cp = seg_copy(i, e, par, base)
                            cp.start()
                            cp.wait()
                    o_ref[...] = o_ref[...] + gate_ref[...] * gather(par, base)

            if final:
                o_ref[...] = _rms(o_ref[...], gf_ref[...])


def _moe_undispatch(yexp, idx2, wts2, x1, gate, g_final, off, cnt, ts, n_experts):
    b, t_len, d = x1.shape
    per_b = t_len // ts
    final = g_final is not None
    in_specs = [
        pl.BlockSpec((ts, 2), lambda i, o, c: (i, 0)),
        pl.BlockSpec((ts, 2), lambda i, o, c: (i, 0)),
        pl.BlockSpec((None, ts, d), lambda i, o, c: (i // per_b, i % per_b, 0)),
        pl.BlockSpec((None, None, 1, d), lambda i, o, c: (i // per_b, 5, 0, 0)),
    ]
    args = [idx2, wts2, x1, gate]
    if final:
        in_specs.append(pl.BlockSpec((1, d), lambda i, o, c: (0, 0)))
        args.append(g_final)
    in_specs.append(pl.BlockSpec(memory_space=pl.ANY))
    args.append(yexp)
    grid_spec = pltpu.PrefetchScalarGridSpec(
        num_scalar_prefetch=2,
        grid=(b * per_b,),
        in_specs=in_specs,
        out_specs=pl.BlockSpec((None, ts, d), lambda i, o, c: (i // per_b, i % per_b, 0)),
        scratch_shapes=[pltpu.VMEM((2, n_experts * SEG_ROWS, d), F32), pltpu.SemaphoreType.DMA((2,))],
    )
    return pl.pallas_call(
        functools.partial(_moe_undispatch_kernel, n_experts=n_experts, final=final),
        grid_spec=grid_spec,
        out_shape=jax.ShapeDtypeStruct((b, t_len, d), F32),
        compiler_params=_cparams(("arbitrary",)),
        name="moe_undispatch",
    )(off, cnt, *args)


def _moe_routed(h2b, idx2, wts2, x1, gate, g_final, wg, wu, wd):
    n, d = h2b.shape
    n_experts = wg.shape[0]
    tm = _moe_tile(2 * n, n_experts)
    off, cnt, tile_expert, tile_valid, ts, ntt, n_tiles = _route_tables(idx2, tm, n_experts)
    xs = _moe_dispatch(h2b, idx2, off, cnt, ts, ntt, n_tiles * tm, n_experts)
    yexp = _moe_grouped(xs, tile_expert, tile_valid, wg, wu, wd, tm)
    return _moe_undispatch(yexp, idx2, wts2, x1, gate, g_final, off, cnt, ts, n_experts)


def _moe_small_kernel(*refs, final):
    if final:
        h_ref, idx_ref, wts_ref, x1_ref, gate_ref, wg_ref, wu_ref, wd_ref, gf_ref, o_ref, acc_ref = refs
    else:
        h_ref, idx_ref, wts_ref, x1_ref, gate_ref, wg_ref, wu_ref, wd_ref, o_ref, acc_ref = refs
    e = pl.program_id(0)
    j = pl.program_id(1)

    @pl.when((e == 0) & (j == 0))
    def _():
        acc_ref[...] = jnp.zeros_like(acc_ref)

    hb = h_ref[...].astype(BF16)
    gp = jnp.dot(hb, wg_ref[...].astype(BF16), preferred_element_type=F32)
    up = jnp.dot(hb, wu_ref[...].astype(BF16), preferred_element_type=F32)
    act = (_silu(gp) * up).astype(BF16)
    y = jnp.dot(act, wd_ref[...].astype(BF16), preferred_element_type=F32)
    idx = idx_ref[...]
    wts = wts_ref[...]
    ge = (jnp.where(idx[:, 0:1] == e, wts[:, 0:1], 0.0) + jnp.where(idx[:, 1:2] == e, wts[:, 1:2], 0.0))
    acc_ref[...] += ge * y

    @pl.when((e == pl.num_programs(0) - 1) & (j == pl.num_programs(1) - 1))
    def _():
        x = x1_ref[...] + gate_ref[...] * acc_ref[...]
        o_ref[...] = _rms(x, gf_ref[...]) if final else x


def _moe_small(h2, idx, wts, x1, gate_rows, wg, wu, wd, g_final):
    rws, d = x1.shape
    n_experts, _, dff = wg.shape
    fc = _ff_chunk(dff)
    final = g_final is not None
    full = lambda e, j: (0, 0)
    in_specs = [
        pl.BlockSpec((rws, d), full), pl.BlockSpec((rws, 2), full), pl.BlockSpec((rws, 2), full),
        pl.BlockSpec((rws, d), full),
        pl.BlockSpec((None, rws, d), lambda e, j: (5, 0, 0)),
        pl.BlockSpec((None, d, fc), lambda e, j: (e, 0, j)),
        pl.BlockSpec((None, d, fc), lambda e, j: (e, 0, j)),
        pl.BlockSpec((None, fc, d), lambda e, j: (e, j, 0)),
    ]
    args = [h2, idx, wts, x1, gate_rows, wg, wu, wd]
    if final:
        in_specs.append(pl.BlockSpec((1, d), full))
        args.append(g_final)
    return pl.pallas_call(
        functools.partial(_moe_small_kernel, final=final),
        grid=(n_experts, dff // fc),
        in_specs=in_specs,
        out_specs=pl.BlockSpec((rws, d), full),
        out_shape=jax.ShapeDtypeStruct((rws, d), F32),
        scratch_shapes=[pltpu.VMEM((rws, d), F32)],
        compiler_params=_cparams(("arbitrary", "arbitrary")),
        name="moe_small",
    )(*args)


def _rope_tables(pos, head_dim):
    half = head_dim // 2
    inv_freq = 1.0 / (ROPE_THETA ** (jnp.arange(half, dtype=F32) / half))
    ang = pos.astype(F32)[:, None] * inv_freq[None, :]
    cos = jnp.tile(jnp.cos(ang), (1, LANES // half))
    sin = jnp.tile(jnp.sin(ang), (1, LANES // half))
    lane = jnp.arange(LANES)
    sign = jnp.where((lane % head_dim) < half, -1.0, 1.0).astype(F32)
    return cos, sin * sign[None, :]


def kernel(x_prompt, x_sample, c_prompt, c_sample, cache_k, cache_v, state_conv, w_ada, b_ada, g_mix, g_ffn, w_in, w_dw, b_dw, g_conv_ln, b_conv_ln, lambda_q1, lambda_k1, lambda_q2, lambda_k2, g_subln, w_out, w_ffn_gate, w_ffn_up, w_ffn_down, w_router, w_moe_gate, w_moe_up, w_moe_down, g_final):
    depth, d, _ = w_in.shape
    bp, tp, _ = x_prompt.shape
    bs, ts, _ = x_sample.shape
    past = cache_k.shape[2]
    n_sub, head_dim = cache_k.shape[3], cache_k.shape[4]
    att_dim = n_sub * head_dim
    width, conv_dim = w_dw.shape[1], w_dw.shape[2]
    n_experts = w_router.shape[2]
    assert tp % Q_BLOCK == 0 and d % LANES == 0 and conv_dim % LANES == 0 and att_dim % LANES == 0
    assert 2 * head_dim == LANES and width - 1 <= CONV_HALO and n_experts <= LANES

    mod_all = _adaln(jnp.concatenate([c_prompt, c_sample], axis=0), w_ada, b_ada)
    cos_p, sin_p = _rope_tables(jnp.arange(tp), head_dim)
    cos_s, sin_s = _rope_tables(past + jnp.arange(ts), head_dim)
    hist_p = jnp.zeros((bp, CONV_HALO, conv_dim), F32)
    pad_rows = CONV_HALO - (width - 1)
    cache_k2 = cache_k.reshape(depth, bs, past, att_dim)
    cache_v2 = cache_v.reshape(depth, bs, past, att_dim)
    g_fin = g_final.reshape(1, d)

    xp, xs = x_prompt, x_sample
    outs = {k: [] for k in ("cp", "cs")}
    kp_all = jnp.zeros((depth, bp, tp, att_dim), F32)
    vp_all = jnp.zeros((depth, bp, tp, att_dim), F32)
    ks_all = jnp.zeros((depth, bs, ts, att_dim), F32)
    vs_all = jnp.zeros((depth, bs, ts, att_dim), F32)
    for l in range(depth):
        last = l == depth - 1
        moe = l % 2 == 1
        lam_init = 0.8 - 0.6 * math.exp(-0.3 * l)
        w_in_b = w_in[l].astype(BF16)
        wo_b = w_out[l].astype(BF16)
        gm = g_mix[l].reshape(1, d)
        gf = g_ffn[l].reshape(1, d)
        lam_args = [a[l].reshape(1, head_dim) for a in (lambda_q1, lambda_k1, lambda_q2, lambda_k2)]
        gsub = g_subln[l].reshape(1, 2 * head_dim)
        conv_args = (w_dw[l], b_dw[l].reshape(1, conv_dim), g_conv_ln[l].reshape(1, conv_dim),
                     b_conv_ln[l].reshape(1, conv_dim))
        mod_p = mod_all[l, :bp].reshape(bp, 6, 1, d)
        mod_s = mod_all[l, bp:].reshape(bs, 6, 1, d)
        mod_s_rows = jnp.broadcast_to(mod_all[l, bp:].reshape(bs, 1, 6, d), (bs, ts, 6, d))
        mod_s_rows = mod_s_rows.transpose(2, 0, 1, 3).reshape(6, 1, bs * ts, d)
        if moe:
            i = l // 2
            wr_b = jnp.pad(w_router[i], ((0, 0), (0, LANES - n_experts))).astype(BF16)
        else:
            i = l // 2
            wr_b = None
            wg_b, wu_b, wd_b = (w_ffn_gate[i].astype(BF16), w_ffn_up[i].astype(BF16), w_ffn_down[i].astype(BF16))

        kp_all, vp_all, cp, qb, kb, vt, cv = _mixer_in(xp, mod_p, gm, w_in_b, cos_p, sin_p, hist_p, *conv_args,
                                                       kp_all, vp_all, layer=l, head_dim=head_dim,
                                                       transpose_v=True)
        att = _attn_prompt(qb, kb, vt, *lam_args, gsub, head_dim=head_dim, lam_init=lam_init)
        res = _out_proj(cv, att, xp, mod_p, mod_p, mod_p, gf, wo_b, wr_b, per_row_mod=False, n_experts=n_experts)
        if moe:
            x1, h2, idx, wts = res
            xp = _moe_routed(h2.reshape(bp * tp, d), idx.reshape(bp * tp, 2), wts.reshape(bp * tp, 2), x1, mod_p,
                             g_fin if last else None, w_moe_gate[i], w_moe_up[i], w_moe_down[i])
        else:
            x1, h2 = res
            xp = _dense_ffn(h2, x1, mod_p, wg_b, wu_b, wd_b, g_fin if last else None, per_row_mod=False)
        outs["cp"].append(cp)

        hist_s = jnp.pad(state_conv[l], ((0, 0), (pad_rows, 0), (0, 0)))
        ks_all, vs_all, cs, qb, kb, vb, cv = _mixer_in(xs, mod_s, gm, w_in_b, cos_s, sin_s, hist_s, *conv_args,
                                                       ks_all, vs_all, layer=l, head_dim=head_dim,
                                                       transpose_v=False)
        att = _attn_sample(qb, kb, vb, cache_k2[l], cache_v2[l], *lam_args, gsub,
                           head_dim=head_dim, lam_init=lam_init)
        rows = bs * ts
        res = _out_proj(cv.reshape(1, rows, conv_dim), att.reshape(1, rows, att_dim), xs.reshape(1, rows, d),
                        mod_s_rows, mod_s_rows, mod_s_rows, gf, wo_b, wr_b, per_row_mod=True, n_experts=n_experts)
        if moe:
            x1, h2, idx, wts = res
            xs = _moe_small(h2.reshape(rows, d), idx.reshape(rows, 2), wts.reshape(rows, 2), x1.reshape(rows, d),
                            mod_s_rows.reshape(6, rows, d), w_moe_gate[i], w_moe_up[i], w_moe_down[i],
                            g_fin if last else None)
        else:
            x1, h2 = res
            xs = _dense_ffn(h2, x1, mod_s_rows, wg_b, wu_b, wd_b, g_fin if last else None, per_row_mod=True)
        xs = xs.reshape(bs, ts, d)
        outs["cs"].append(cs)

    return (xp, xs,
            kp_all.reshape(depth, bp, tp, n_sub, head_dim), vp_all.reshape(depth, bp, tp, n_sub // 2, 2 * head_dim),
            jnp.stack(outs["cp"]),
            ks_all.reshape(depth, bs, ts, n_sub, head_dim), vs_all.reshape(depth, bs, ts, n_sub // 2, 2 * head_dim),
            jnp.stack(outs["cs"]))
```
